```python
import math
import jax, jax.numpy as jnp
from jax import lax
import numpy as np

D_MODEL = 1024
BATCH = 1
SEQ = 16384
DEPTH = 1
DEC_BATCH = 32
DEC_SEQ = 8
PAST_LEN = 16384
PAGE_SIZE = 128

H_DIFF = D_MODEL // 128
D_QK = 64
D_V = 2 * D_QK
Q_BLOCK = 128
H_RET = 4
DK_RET = D_MODEL // H_RET
DV_RET = D_MODEL // H_RET
RET_CHUNK = 128
ROPE_BASE = 10000.0
N_MEM = 256
H_MEM = 4
D_MEM = D_MODEL // H_MEM
N_BRANCH = 3
N_EXPERTS = 32
TOP_K = 4
D_FF = D_MODEL
SWIGLU_LIMIT = 7.0
SWIGLU_ALPHA = 1.702
MOE_BLOCK = 128
RMS_EPS = 1e-6
IN_SPLITS = (H_DIFF * 2 * D_QK, H_DIFF * 2 * D_QK, H_DIFF * D_V,
             H_RET * DK_RET, H_RET * DK_RET, H_RET * DV_RET, H_RET * DV_RET,
             H_MEM * D_MEM, N_BRANCH * D_MODEL)
D_IN = sum(IN_SPLITS)

kernel_name = 'hybrid_diffattn_retention_moe_decode_step'


def rms_norm(x, g=None):
    xf = x.astype(jnp.float32)
    y = xf * lax.rsqrt(jnp.mean(xf * xf, axis=-1, keepdims=True) + RMS_EPS)
    if g is not None:
        y = y * g.astype(jnp.float32)
    return y.astype(x.dtype)


def split_projection(xn, w_in):
    lead = xn.shape[:-1]
    offs = [int(o) for o in np.cumsum(IN_SPLITS)[:-1]]
    dq, dk, dv, rq, rk, rv, rg, mq, gt = jnp.split(xn @ w_in, offs, axis=-1)
    return (dq.reshape(lead + (H_DIFF, 2, D_QK)),
            dk.reshape(lead + (H_DIFF, 2, D_QK)),
            dv.reshape(lead + (H_DIFF, D_V)),
            rq.reshape(lead + (H_RET, DK_RET)),
            rk.reshape(lead + (H_RET, DK_RET)),
            rv.reshape(lead + (H_RET, DV_RET)),
            rg,
            mq.reshape(lead + (H_MEM, D_MEM)),
            gt.reshape(lead + (N_BRANCH, D_MODEL)))


def diff_lambda(lq1, lk1, lq2, lk2, lam_init):
    f = lambda a, b: jnp.exp(jnp.sum(a.astype(jnp.float32) * b.astype(jnp.float32)))
    return f(lq1, lk1) - f(lq2, lk2) + lam_init


def diff_attn_prompt(q, k, v, lam):
    b, s = q.shape[:2]
    nb = s // Q_BLOCK
    scale = D_QK ** -0.5
    kpos = jnp.arange(s)
    qb = jnp.moveaxis(q.reshape((b, nb, Q_BLOCK) + q.shape[2:]), 1, 0)
    qpos = kpos.reshape(nb, Q_BLOCK)

    def one_block(args):
        qi, pos = args
        sc = jnp.einsum('bqhcd,bkhcd->bhcqk', qi, k).astype(jnp.float32) * scale
        sc = jnp.where(kpos[None, :] <= pos[:, None], sc, -jnp.inf)
        p = jax.nn.softmax(sc, axis=-1)
        a = p[:, :, 0] - lam * p[:, :, 1]
        return jnp.einsum('bhqk,bkhv->bqhv', a.astype(v.dtype), v)

    o = lax.map(one_block, (qb, qpos))
    return jnp.moveaxis(o, 0, 1).reshape(b, s, H_DIFF, D_V)


def diff_attn_sample(q, k_new, v_new, k_past, v_past, lam):
    t = q.shape[1]
    p_len = k_past.shape[1]
    scale = D_QK ** -0.5
    s_past = jnp.einsum('bqhcd,bkhcd->bhcqk', q, k_past).astype(jnp.float32) * scale
    s_new = jnp.einsum('bqhcd,bkhcd->bhcqk', q, k_new).astype(jnp.float32) * scale
    causal = jnp.tril(jnp.ones((t, t), dtype=bool))
    s_new = jnp.where(causal, s_new, -jnp.inf)
    p = jax.nn.softmax(jnp.concatenate([s_past, s_new], axis=-1), axis=-1)
    a = p[:, :, 0] - lam * p[:, :, 1]
    return (jnp.einsum('bhqk,bkhv->bqhv', a[..., :p_len].astype(v_past.dtype), v_past)
            + jnp.einsum('bhqk,bkhv->bqhv', a[..., p_len:].astype(v_new.dtype), v_new))


def diff_post(o, subln_g, lam_init):
    y = rms_norm(o, subln_g) * (1.0 - lam_init)
    return y.reshape(o.shape[:-2] + (H_DIFF * D_V,))


def retention_log_decay():
    return jnp.log(1.0 - jnp.exp2(-5.0 - jnp.arange(H_RET, dtype=jnp.float32)))


def rope(x, pos):
    half = x.shape[-1] // 2
    inv = ROPE_BASE ** (-jnp.arange(half, dtype=jnp.float32) / half)
    ang = pos.astype(jnp.float32)[:, None] * inv[None, :]
    cos = jnp.cos(ang)[:, None, :]
    sin = jnp.sin(ang)[:, None, :]
    xf = x.astype(jnp.float32)
    x1, x2 = xf[..., :half], xf[..., half:]
    return jnp.concatenate([x1 * cos - x2 * sin, x1 * sin + x2 * cos], axis=-1)


def retention_qk(q, k, pos):
    return rope(q, pos), rope(k, pos) * (DK_RET ** -0.5)


def retention_chunk(state, q, k, v, log_g):
    c = q.shape[1]
    idx = jnp.arange(c, dtype=jnp.float32)
    dist = idx[:, None] - idx[None, :]
    decay = jnp.where(dist >= 0, jnp.exp(jnp.maximum(dist, 0.0)[None] * log_g[:, None, None]), 0.0)
    sc = jnp.einsum('bihd,bjhd->bhij', q, k) * decay[None]
    o = jnp.einsum('bhij,bjhv->bihv', sc, v)
    o = o + jnp.einsum('bihd,bhdv->bihv', q, state) * jnp.exp((idx[:, None] + 1.0) * log_g[None, :])[None, :, :, None]
    kw = k * jnp.exp((c - 1.0 - idx)[:, None] * log_g[None, :])[None, :, :, None]
    new_state = jnp.exp(c * log_g)[None, :, None, None] * state + jnp.einsum('bjhd,bjhv->bhdv', kw, v)
    return new_state, o


def retention_prompt(q, k, v, log_g):
    b, s = q.shape[:2]
    nc = s // RET_CHUNK

    def to_chunks(a):
        return jnp.moveaxis(a.reshape((b, nc, RET_CHUNK) + a.shape[2:]), 1, 0)

    s0 = jnp.zeros((b, H_RET, DK_RET, DV_RET), jnp.float32)
    state, o = lax.scan(lambda st, xs: retention_chunk(st, xs[0], xs[1], xs[2], log_g),
                        s0, (to_chunks(q), to_chunks(k), to_chunks(v)))
    return jnp.moveaxis(o, 0, 1).reshape(b, s, H_RET, DV_RET), state


def retention_post(o, g):
    on = rms_norm(o).reshape(o.shape[:-2] + (H_RET * DV_RET,))
    return (jax.nn.silu(g.astype(jnp.float32)) * on.astype(jnp.float32)).astype(g.dtype)


def memory_kv(mem, norm_mem_g, w_mem_kv):
    b, m = mem.shape[:2]
    k, v = jnp.split(rms_norm(mem, norm_mem_g) @ w_mem_kv, 2, axis=-1)
    return k.reshape(b, m, H_MEM, D_MEM), v.reshape(b, m, H_MEM, D_MEM)


def memory_attn(q, mk, mv):
    sc = jnp.einsum('bthd,bmhd->bhtm', q, mk).astype(jnp.float32) * (D_MEM ** -0.5)
    p = jax.nn.softmax(sc, axis=-1)
    o = jnp.einsum('bhtm,bmhd->bthd', p.astype(mv.dtype), mv)
    return o.reshape(o.shape[:-2] + (H_MEM * D_MEM,))


def merge_branches(y_d, y_r, y_m, gates, w_branch, w_out):
    ys = jnp.stack([y_d, y_r, y_m], axis=-2).astype(w_branch.dtype)
    proj = jnp.einsum('...id,ide->...ie', ys, w_branch)
    mixed = jnp.sum(jax.nn.sigmoid(gates.astype(jnp.float32)) * proj.astype(jnp.float32), axis=-2)
    return mixed.astype(w_out.dtype) @ w_out


def moe_ffn(h, router_w, router_b, w1, b1, w2, b2):
    n, d = h.shape
    logits = (h @ router_w).astype(jnp.float32) + router_b.astype(jnp.float32)
    top_val, top_idx = lax.top_k(logits, TOP_K)
    gate = jax.nn.softmax(top_val, axis=-1)
    n_assign = n * TOP_K
    n_blocks = -(-(n_assign + N_EXPERTS * (MOE_BLOCK - 1)) // MOE_BLOCK)
    cap = n_blocks * MOE_BLOCK
    e_flat = top_idx.reshape(-1).astype(jnp.int32)
    tok_flat = jnp.repeat(jnp.arange(n, dtype=jnp.int32), TOP_K)
    g_flat = gate.reshape(-1)
    order = jnp.argsort(e_flat)
    e_sorted = e_flat[order]
    counts = jnp.bincount(e_flat, length=N_EXPERTS)
    starts = jnp.cumsum(counts) - counts
    padded = ((counts + MOE_BLOCK - 1) // MOE_BLOCK) * MOE_BLOCK
    pad_ends = jnp.cumsum(padded)
    pad_starts = pad_ends - padded
    dest = pad_starts[e_sorted] + (jnp.arange(n_assign) - starts[e_sorted])
    slot_tok = jnp.full((cap,), n, jnp.int32).at[dest].set(tok_flat[order])
    slot_gate = jnp.zeros((cap,), jnp.float32).at[dest].set(g_flat[order])
    block_exp = jnp.minimum(jnp.searchsorted(pad_ends, jnp.arange(n_blocks) * MOE_BLOCK, side='right'),
                            N_EXPERTS - 1)
    h_pad = jnp.concatenate([h, jnp.zeros((1, d), h.dtype)], axis=0)

    def run_block(args):
        toks, e = args
        u = h_pad[toks] @ w1[e] + b1[e]
        x_glu = jnp.minimum(u[:, 0::2], SWIGLU_LIMIT)
        x_lin = jnp.clip(u[:, 1::2], -SWIGLU_LIMIT, SWIGLU_LIMIT)
        act = x_glu * jax.nn.sigmoid(SWIGLU_ALPHA * x_glu) * (x_lin + 1.0)
        return act @ w2[e] + b2[e]

    out = lax.map(run_block, (slot_tok.reshape(n_blocks, MOE_BLOCK), block_exp))
    out = out.reshape(cap, d).astype(jnp.float32) * slot_gate[:, None]
    y = jax.ops.segment_sum(out, slot_tok, num_segments=n + 1)[:n]
    return y.astype(h.dtype)


def channel_mixer(h, norm2_g, router_w, router_b, w1, b1, w2, b2):
    hn = rms_norm(h, norm2_g)
    y = moe_ffn(hn.reshape(-1, hn.shape[-1]), router_w, router_b, w1, b1, w2, b2)
    return h + y.reshape(h.shape)


def setup_inputs(seed: int = 0) -> dict:
    key = jax.random.key(seed)
    keys = iter(jax.random.split(key, 40))
    nrm = lambda shape, scale=1.0: jax.random.normal(next(keys), shape, jnp.float32) * scale
    n_pages = PAST_LEN // PAGE_SIZE
    n_used = DEC_BATCH * n_pages
    n_pool = n_used + max(1, n_used // 4)
    page_table = jax.random.permutation(next(keys), n_pool)[:n_used].reshape(DEC_BATCH, n_pages).astype(jnp.int32)
    return {
        'x_prompt': nrm((BATCH, SEQ, D_MODEL)),
        'x_sample': nrm((DEC_BATCH, DEC_SEQ, D_MODEL)),
        'cache_k': nrm((DEPTH, n_pool, PAGE_SIZE, H_DIFF, 2 * D_QK)),
        'cache_v': nrm((DEPTH, n_pool, PAGE_SIZE, H_DIFF, D_V)),
        'state_ret': nrm((DEPTH, DEC_BATCH, H_RET, DK_RET, DV_RET), 0.5),
        'cache_mem_k': nrm((DEPTH, DEC_BATCH, N_MEM, H_MEM, D_MEM)),
        'cache_mem_v': nrm((DEPTH, DEC_BATCH, N_MEM, H_MEM, D_MEM)),
        'page_table': page_table,
        'mem_prompt': nrm((BATCH, N_MEM, D_MODEL)),
        'norm1_g': 1.0 + nrm((DEPTH, D_MODEL), 0.05),
        'w_in': nrm((DEPTH, D_MODEL, D_IN), D_MODEL ** -0.5),
        'lambda_q1': nrm((DEPTH, D_QK), 0.1),
        'lambda_k1': nrm((DEPTH, D_QK), 0.1),
        'lambda_q2': nrm((DEPTH, D_QK), 0.1),
        'lambda_k2': nrm((DEPTH, D_QK), 0.1),
        'subln_g': 1.0 + nrm((DEPTH, D_V), 0.05),
        'norm_mem_g': 1.0 + nrm((DEPTH, D_MODEL), 0.05),
        'w_mem_kv': nrm((DEPTH, D_MODEL, 2 * H_MEM * D_MEM), D_MODEL ** -0.5),
        'w_branch': nrm((DEPTH, N_BRANCH, D_MODEL, D_MODEL), D_MODEL ** -0.5),
        'w_out': nrm((DEPTH, D_MODEL, D_MODEL), D_MODEL ** -0.5),
        'norm2_g': 1.0 + nrm((DEPTH, D_MODEL), 0.05),
        'router_w': nrm((DEPTH, D_MODEL, N_EXPERTS), D_MODEL ** -0.5),
        'router_b': nrm((DEPTH, N_EXPERTS), 0.01),
        'w1': nrm((DEPTH, N_EXPERTS, D_MODEL, 2 * D_FF), D_MODEL ** -0.5),
        'b1': nrm((DEPTH, N_EXPERTS, 2 * D_FF), 0.01),
        'w2': nrm((DEPTH, N_EXPERTS, D_FF, D_MODEL), D_FF ** -0.5),
        'b2': nrm((DEPTH, N_EXPERTS, D_MODEL), 0.01),
        'normf_g': 1.0 + nrm((D_MODEL,), 0.05),
    }


def reference(x_prompt, x_sample, cache_k, cache_v, state_ret, cache_mem_k, cache_mem_v, page_table,
              mem_prompt, norm1_g, w_in, lambda_q1, lambda_k1, lambda_q2, lambda_k2, subln_g,
              norm_mem_g, w_mem_kv, w_branch, w_out, norm2_g, router_w, router_b, w1, b1, w2, b2,
              normf_g):
    b, s = x_prompt.shape[:2]
    db, t = x_sample.shape[:2]
    pos_p = jnp.arange(s)
    pos_s = PAST_LEN + jnp.arange(t)
    log_g = retention_log_decay()
    hp, hs = x_prompt, x_sample
    k_p_rows, v_p_rows, st_p_all, mk_all, mv_all = [], [], [], [], []
    k_s_rows, v_s_rows, st_s_all = [], [], []
    for l in range(DEPTH):
        lam_init = 0.8 - 0.6 * math.exp(-0.3 * l)
        lam = diff_lambda(lambda_q1[l], lambda_k1[l], lambda_q2[l], lambda_k2[l], lam_init)

        dq, dk, dv, rq, rk, rv, rg, mq, gt = split_projection(rms_norm(hp, norm1_g[l]), w_in[l])
        y_d = diff_post(diff_attn_prompt(dq, dk, dv, lam), subln_g[l], lam_init)
        rq_r, rk_r = retention_qk(rq, rk, pos_p)
        o_r, st_p = retention_prompt(rq_r, rk_r, rv.astype(jnp.float32), log_g)
        y_r = retention_post(o_r, rg)
        mk, mv = memory_kv(mem_prompt, norm_mem_g[l], w_mem_kv[l])
        y_m = memory_attn(mq, mk, mv)
        hp = hp + merge_branches(y_d, y_r, y_m, gt, w_branch[l], w_out[l])
        hp = channel_mixer(hp, norm2_g[l], router_w[l], router_b[l], w1[l], b1[l], w2[l], b2[l])
        k_p_rows.append(dk.reshape(b, s, H_DIFF, 2 * D_QK))
        v_p_rows.append(dv)
        st_p_all.append(st_p)
        mk_all.append(mk)
        mv_all.append(mv)

        dq, dk, dv, rq, rk, rv, rg, mq, gt = split_projection(rms_norm(hs, norm1_g[l]), w_in[l])
        k_past = cache_k[l][page_table].reshape(db, -1, H_DIFF, 2, D_QK)
        v_past = cache_v[l][page_table].reshape(db, -1, H_DIFF, D_V)
        y_d = diff_post(diff_attn_sample(dq, dk, dv, k_past, v_past, lam), subln_g[l], lam_init)
        rq_r, rk_r = retention_qk(rq, rk, pos_s)
        st_s, o_r = retention_chunk(state_ret[l].astype(jnp.float32), rq_r, rk_r, rv.astype(jnp.float32), log_g)
        y_r = retention_post(o_r, rg)
        y_m = memory_attn(mq, cache_mem_k[l], cache_mem_v[l])
        hs = hs + merge_branches(y_d, y_r, y_m, gt, w_branch[l], w_out[l])
        hs = channel_mixer(hs, norm2_g[l], router_w[l], router_b[l], w1[l], b1[l], w2[l], b2[l])
        k_s_rows.append(dk.reshape(db, t, H_DIFF, 2 * D_QK))
        v_s_rows.append(dv)
        st_s_all.append(st_s.astype(state_ret.dtype))

    y_prompt = rms_norm(hp, normf_g)
    y_sample = rms_norm(hs, normf_g)
    return (y_prompt, y_sample, jnp.stack(k_p_rows), jnp.stack(v_p_rows), jnp.stack(st_p_all),
            jnp.stack(mk_all), jnp.stack(mv_all), jnp.stack(k_s_rows), jnp.stack(v_s_rows),
            jnp.stack(st_s_all))
```

```python
import functools
import math

import jax
import jax.numpy as jnp
from jax import lax
from jax.experimental import pallas as pl
from jax.experimental.pallas import tpu as pltpu

F32 = jnp.float32
BF16 = jnp.bfloat16
I32 = jnp.int32

H_DIFF = 8
D_QK = 64
D_V = 128
H_RET = 4
H_MEM = 4
N_BRANCH = 3
N_EXPERTS = 32
TOP_K = 4
SWIGLU_LIMIT = 7.0
SWIGLU_ALPHA = 1.702
ROPE_BASE = 10000.0
RMS_EPS = 1e-6

V7X_LANES = 128
V7X_VMEM_LIMIT_BYTES = 56 * 1024 * 1024

MOE_BLOCK = 256
NEG_INF = float("-inf")


def _cparams(sem):
    return pltpu.CompilerParams(dimension_semantics=sem, vmem_limit_bytes=V7X_VMEM_LIMIT_BYTES)


def _rms(x, g=None):
    y = x * lax.rsqrt(jnp.mean(x * x, axis=-1, keepdims=True) + RMS_EPS)
    return y if g is None else y * g


def _row_tile(n, pref):
    t = min(n, pref)
    while n % t:
        t //= 2
    return t


def _proj_diff_kernel(x_ref, g_ref, w_ref, q_ref, kf_ref, vf_ref, kb_ref, vb_ref):
    xn = _rms(x_ref[...], g_ref[...]).astype(BF16)
    d = kf_ref.shape[-1]
    q = jnp.dot(xn, w_ref[:, 0:d], preferred_element_type=F32)
    q_ref[...] = (q * (D_QK ** -0.5)).astype(q_ref.dtype)
    k = jnp.dot(xn, w_ref[:, d:2 * d], preferred_element_type=F32)
    kf_ref[...] = k
    kb_ref[...] = k.astype(BF16)
    v = jnp.dot(xn, w_ref[:, 2 * d:3 * d], preferred_element_type=F32)
    vf_ref[...] = v
    vb_ref[...] = v.astype(BF16)


def _proj_diff(x, g, w, q_dtype):
    t, dm = x.shape
    tm = _row_tile(t, 512)
    row = lambda i: (i, 0)
    blk = pl.BlockSpec((tm, dm), row)
    return pl.pallas_call(
        _proj_diff_kernel,
        grid=(t // tm,),
        in_specs=[blk, pl.BlockSpec((1, dm), lambda i: (0, 0)), pl.BlockSpec(w.shape, lambda i: (0, 0))],
        out_specs=[blk] * 5,
        out_shape=[jax.ShapeDtypeStruct((t, dm), q_dtype), jax.ShapeDtypeStruct((t, dm), F32),
                   jax.ShapeDtypeStruct((t, dm), F32), jax.ShapeDtypeStruct((t, dm), BF16),
                   jax.ShapeDtypeStruct((t, dm), BF16)],
        compiler_params=_cparams(("parallel",)),
        name="proj_diff",
    )(x, g, w)


def _proj_ret_kernel(x_ref, g_ref, w_ref, cos_ref, sin_ref, q_ref, k_ref, v_ref, gate_ref):
    xn = _rms(x_ref[...], g_ref[...]).astype(BF16)
    d = q_ref.shape[-1]
    dk = d // H_RET
    half = dk // 2
    cos = cos_ref[...]
    sin = sin_ref[...]

    def rope(u, scale):
        outs = []
        for h in range(H_RET):
            x1 = u[:, h * dk:h * dk + half]
            x2 = u[:, h * dk + half:(h + 1) * dk]
            outs.append((x1 * cos - x2 * sin) * scale)
            outs.append((x1 * sin + x2 * cos) * scale)
        return jnp.concatenate(outs, axis=-1)

    q_ref[...] = rope(jnp.dot(xn, w_ref[:, 0:d], preferred_element_type=F32), 1.0)
    k_ref[...] = rope(jnp.dot(xn, w_ref[:, d:2 * d], preferred_element_type=F32), dk ** -0.5)
    v_ref[...] = jnp.dot(xn, w_ref[:, 2 * d:3 * d], preferred_element_type=F32)
    gate_ref[...] = jnp.dot(xn, w_ref[:, 3 * d:4 * d], preferred_element_type=F32)


def _proj_ret(x, g, w, cos, sin):
    t, dm = x.shape
    tm = _row_tile(t, 512)
    row = lambda i: (i, 0)
    blk = pl.BlockSpec((tm, dm), row)
    tab = pl.BlockSpec((tm, cos.shape[1]), row)
    return pl.pallas_call(
        _proj_ret_kernel,
        grid=(t // tm,),
        in_specs=[blk, pl.BlockSpec((1, dm), lambda i: (0, 0)), pl.BlockSpec(w.shape, lambda i: (0, 0)), tab, tab],
        out_specs=[blk] * 4,
        out_shape=[jax.ShapeDtypeStruct((t, dm), F32)] * 4,
        compiler_params=_cparams(("parallel",)),
        name="proj_ret",
    )(x, g, w, cos, sin)


def _proj_mem_gate_kernel(x_ref, g_ref, w_ref, mq_ref, gt_ref):
    xn = _rms(x_ref[...], g_ref[...]).astype(BF16)
    d = mq_ref.shape[-1]
    mq = jnp.dot(xn, w_ref[:, 0:d], preferred_element_type=F32)
    mq_ref[...] = (mq * ((d // H_MEM) ** -0.5)).astype(mq_ref.dtype)
    gt_ref[...] = jnp.dot(xn, w_ref[:, d:], preferred_element_type=F32)


def _proj_mem_gate(x, g, w, mq_dtype):
    t, dm = x.shape
    tm = _row_tile(t, 512)
    row = lambda i: (i, 0)
    return pl.pallas_call(
        _proj_mem_gate_kernel,
        grid=(t // tm,),
        in_specs=[pl.BlockSpec((tm, dm), row), pl.BlockSpec((1, dm), lambda i: (0, 0)),
                  pl.BlockSpec(w.shape, lambda i: (0, 0))],
        out_specs=[pl.BlockSpec((tm, dm), row), pl.BlockSpec((tm, N_BRANCH * dm), row)],
        out_shape=[jax.ShapeDtypeStruct((t, dm), mq_dtype), jax.ShapeDtypeStruct((t, N_BRANCH * dm), F32)],
        compiler_params=_cparams(("parallel",)),
        name="proj_mem_gate",
    )(x, g, w)


def _lambda_value(lam_ref, lam_init):
    t = lam_ref[...]
    a = jnp.sum(t[0:1] * t[1:2], axis=-1, keepdims=True)
    b = jnp.sum(t[2:3] * t[3:4], axis=-1, keepdims=True)
    return jnp.exp(a) - jnp.exp(b) + lam_init


def _diff_attn_prompt_kernel(q_ref, k_ref, v_ref, lam_ref, g_ref, o_ref, m_s, l_s, acc_s, *, tq, tk, lam_init):
    qi = pl.program_id(1)
    q = q_ref[...]
    lane = lax.broadcasted_iota(I32, q.shape, 1)
    zero = jnp.zeros_like(q)
    qq = jnp.concatenate([jnp.where(lane < D_QK, q, zero), jnp.where(lane >= D_QK, q, zero)], axis=0)
    m_s[...] = jnp.full(m_s.shape, NEG_INF, F32)
    l_s[...] = jnp.zeros(l_s.shape, F32)
    acc_s[...] = jnp.zeros(acc_s.shape, F32)

    def step(j, masked):
        start = pl.multiple_of(j * tk, tk)
        ks = k_ref[pl.ds(start, tk), :]
        vs = v_ref[pl.ds(start, tk), :]
        s = lax.dot_general(qq, ks, (((1,), (1,)), ((), ())), preferred_element_type=F32)
        if masked:
            row = lax.broadcasted_iota(I32, s.shape, 0)
            qpos = jnp.where(row >= tq, row - tq, row) + qi * tq
            kpos = lax.broadcasted_iota(I32, s.shape, 1) + j * tk
            s = jnp.where(kpos <= qpos, s, NEG_INF)
        m_old = m_s[...]
        m_new = jnp.maximum(m_old, jnp.max(s, axis=-1, keepdims=True))
        alpha = jnp.exp(m_old - m_new)
        p = jnp.exp(s - m_new)
        l_s[...] = alpha * l_s[...] + jnp.sum(p, axis=-1, keepdims=True)
        acc_s[...] = alpha * acc_s[...] + jnp.dot(p.astype(BF16), vs, preferred_element_type=F32)
        m_s[...] = m_new

    n_full = (qi * tq) // tk

    def body(j, c):
        step(j, False)
        return c

    lax.fori_loop(0, n_full, body, 0)
    step(n_full, True)

    o = acc_s[...] / l_s[...]
    lam = _lambda_value(lam_ref, lam_init)
    od = o[:tq] - lam * o[tq:]
    o_ref[...] = (_rms(od, g_ref[...]) * (1.0 - lam_init)).astype(o_ref.dtype)


def _diff_attn_prompt(q, k, v, lam4, subln_g, lam_init):
    s, dm = q.shape
    tq = _row_tile(s, 256)
    tk = _row_tile(s, 512)
    kern = functools.partial(_diff_attn_prompt_kernel, tq=tq, tk=tk, lam_init=lam_init)
    return pl.pallas_call(
        kern,
        grid=(H_DIFF, s // tq),
        in_specs=[pl.BlockSpec((tq, D_V), lambda h, i: (i, h)),
                  pl.BlockSpec((s, D_V), lambda h, i: (0, h)),
                  pl.BlockSpec((s, D_V), lambda h, i: (0, h)),
                  pl.BlockSpec(lam4.shape, lambda h, i: (0, 0)),
                  pl.BlockSpec((1, D_V), lambda h, i: (0, 0))],
        out_specs=pl.BlockSpec((tq, D_V), lambda h, i: (i, h)),
        out_shape=jax.ShapeDtypeStruct((s, dm), BF16),
        scratch_shapes=[pltpu.VMEM((2 * tq, 1), F32), pltpu.VMEM((2 * tq, 1), F32),
                        pltpu.VMEM((2 * tq, D_V), F32)],
        compiler_params=_cparams(("parallel", "parallel")),
        name="diff_attn_prompt",
    )(q, k, v, lam4, subln_g)


def _diff_attn_sample_kernel(pt_ref, q_ref, kn_ref, vn_ref, lam_ref, g_ref, *rest, pps, lam_init):
    k_refs = rest[:pps]
    v_refs = rest[pps:2 * pps]
    o_ref, qmat_s, bias_s, m_s, l_s, acc_s = rest[2 * pps:]
    j = pl.program_id(1)
    nq = q_ref.shape[1]
    rows = H_DIFF * 2 * nq

    @pl.when(j == 0)
    def _():
        qb = q_ref[0]
        lane = lax.broadcasted_iota(I32, (nq, D_V), 1)
        parts = []
        for h in range(H_DIFF):
            blk = qb[:, h * D_V:(h + 1) * D_V]
            parts.append(jnp.where(lane < D_QK, blk, 0.0))
            parts.append(jnp.where(lane >= D_QK, blk, 0.0))
        qmat_s[...] = jnp.concatenate(parts, axis=0).astype(BF16)
        r = lax.broadcasted_iota(I32, bias_s.shape, 0)
        c = lax.broadcasted_iota(I32, bias_s.shape, 1)
        bias_s[...] = jnp.where((c % H_DIFF) == (r // (2 * nq)), 0.0, NEG_INF)
        m_s[...] = jnp.full(m_s.shape, NEG_INF, F32)
        l_s[...] = jnp.zeros(l_s.shape, F32)
        acc_s[...] = jnp.zeros(acc_s.shape, F32)

    def update(s, vflat):
        m_old = m_s[...]
        m_new = jnp.maximum(m_old, jnp.max(s, axis=-1, keepdims=True))
        alpha = jnp.exp(m_old - m_new)
        p = jnp.exp(s - m_new)
        l_s[...] = alpha * l_s[...] + jnp.sum(p, axis=-1, keepdims=True)
        acc_s[...] = alpha * acc_s[...] + jnp.dot(p.astype(BF16), vflat, preferred_element_type=F32)
        m_s[...] = m_new

    qmat = qmat_s[...]
    for r in range(pps):
        kp = k_refs[r][0]
        kflat = kp.reshape(kp.shape[0] * kp.shape[1], kp.shape[2]).astype(BF16)
        vp = v_refs[r][0]
        vflat = vp.reshape(vp.shape[0] * vp.shape[1], vp.shape[2]).astype(BF16)
        s = lax.dot_general(qmat, kflat, (((1,), (1,)), ((), ())), preferred_element_type=F32)
        update(s + bias_s[...], vflat)

    @pl.when(j == pl.num_programs(1) - 1)
    def _():
        kn = kn_ref[0]
        knf = kn.reshape(kn.shape[0] * kn.shape[1], kn.shape[2]).astype(BF16)
        vn = vn_ref[0]
        vnf = vn.reshape(vn.shape[0] * vn.shape[1], vn.shape[2]).astype(BF16)
        s = lax.dot_general(qmat, knf, (((1,), (1,)), ((), ())), preferred_element_type=F32)
        r = lax.broadcasted_iota(I32, s.shape, 0)
        c = lax.broadcasted_iota(I32, s.shape, 1)
        keep = ((c % H_DIFF) == (r // (2 * nq))) & ((c // H_DIFF) <= (r % nq))
        update(jnp.where(keep, s, NEG_INF), vnf)
        o = acc_s[...] / l_s[...]
        lam = _lambda_value(lam_ref, lam_init)
        g = g_ref[...]
        outs = []
        for h in range(H_DIFF):
            base = h * 2 * nq
            od = o[base:base + nq] - lam * o[base + nq:base + 2 * nq]
            outs.append(_rms(od, g) * (1.0 - lam_init))
        o_ref[0] = jnp.concatenate(outs, axis=-1)


def _diff_attn_sample(page_table, q, k_new, v_new, cache_k, cache_v, lam4, subln_g, lam_init, pps=4):
    b, nq, dm = q.shape
    n_pages = page_table.shape[1]
    while n_pages % pps:
        pps //= 2
    page_shape = (1,) + cache_k.shape[1:]
    page_specs = [pl.BlockSpec(page_shape, functools.partial(
        lambda bi, j, pt, r: (pt[bi, j * pps + r], 0, 0, 0), r=r)) for r in range(pps)]
    rows = H_DIFF * 2 * nq
    cols = cache_k.shape[1] * cache_k.shape[2]
    kern = functools.partial(_diff_attn_sample_kernel, pps=pps, lam_init=lam_init)
    per_seq3 = lambda bi, j, pt: (bi, 0, 0)
    per_seq4 = lambda bi, j, pt: (bi, 0, 0, 0)
    const2 = lambda bi, j, pt: (0, 0)
    grid_spec = pltpu.PrefetchScalarGridSpec(
        num_scalar_prefetch=1,
        grid=(b, n_pages // pps),
        in_specs=[pl.BlockSpec((1, nq, dm), per_seq3),
                  pl.BlockSpec((1,) + k_new.shape[1:], per_seq4),
                  pl.BlockSpec((1,) + v_new.shape[1:], per_seq4),
                  pl.BlockSpec(lam4.shape, const2),
                  pl.BlockSpec((1, D_V), const2)] + page_specs + page_specs,
        out_specs=pl.BlockSpec((1, nq, dm), per_seq3),
        scratch_shapes=[pltpu.VMEM((rows, D_V), BF16), pltpu.VMEM((rows, cols), F32),
                        pltpu.VMEM((rows, 1), F32), pltpu.VMEM((rows, 1), F32), pltpu.VMEM((rows, D_V), F32)],
    )
    return pl.pallas_call(
        kern,
        grid_spec=grid_spec,
        out_shape=jax.ShapeDtypeStruct((b, nq, dm), F32),
        compiler_params=_cparams(("parallel", "arbitrary")),
        name="diff_attn_sample",
    )(page_table, q, k_new, v_new, lam4, subln_g, *([cache_k] * pps), *([cache_v] * pps))


def _retention_kernel(q_ref, k_ref, v_ref, gate_ref, s0_ref, y_ref, sout_ref, state_s, *, mm_dtype):
    h = pl.program_id(1)
    c = pl.program_id(2)

    @pl.when(c == 0)
    def _():
        state_s[...] = s0_ref[0, 0]

    q = q_ref[0]
    k = k_ref[0]
    v = v_ref[0]
    cl = q.shape[0]
    hf = jnp.full((1, 1), h, I32).astype(F32)
    log_g = jnp.log(1.0 - jnp.exp2(-5.0 - hf))
    ii = lax.broadcasted_iota(I32, (cl, cl), 0)
    jj = lax.broadcasted_iota(I32, (cl, cl), 1)
    dist = (ii - jj).astype(F32)
    decay = jnp.where(dist >= 0, jnp.exp(jnp.maximum(dist, 0.0) * log_g), 0.0)
    ri = lax.broadcasted_iota(I32, (cl, 1), 0).astype(F32)
    row_decay = jnp.exp((ri + 1.0) * log_g)
    col_decay = jnp.exp((cl - 1.0 - ri) * log_g)

    qm = q.astype(mm_dtype)
    vm = v.astype(mm_dtype)
    state = state_s[...]
    sc = lax.dot_general(qm, k.astype(mm_dtype), (((1,), (1,)), ((), ())), preferred_element_type=F32) * decay
    o = jnp.dot(sc.astype(mm_dtype), vm, preferred_element_type=F32)
    o = o + jnp.dot(qm, state.astype(mm_dtype), preferred_element_type=F32) * row_decay
    kw = (k * col_decay).astype(mm_dtype)
    upd = lax.dot_general(kw, vm, (((0,), (0,)), ((), ())), preferred_element_type=F32)
    new_state = jnp.exp(cl * log_g) * state + upd
    state_s[...] = new_state

    g = gate_ref[0]
    y_ref[0] = (g * jax.nn.sigmoid(g) * _rms(o)).astype(y_ref.dtype)

    @pl.when(c == pl.num_programs(2) - 1)
    def _():
        sout_ref[0, 0] = new_state


def _retention(q, k, v, gate, state0, chunk, y_dtype, mm_dtype):
    b, s, dm = q.shape
    dk = dm // H_RET
    nc = s // chunk
    blk = pl.BlockSpec((1, chunk, dk), lambda bi, h, c: (bi, c, h))
    st = pl.BlockSpec((1, 1, dk, dk), lambda bi, h, c: (bi, h, 0, 0))
    return pl.pallas_call(
        functools.partial(_retention_kernel, mm_dtype=mm_dtype),
        grid=(b, H_RET, nc),
        in_specs=[blk, blk, blk, blk, st],
        out_specs=[blk, st],
        out_shape=[jax.ShapeDtypeStruct((b, s, dm), y_dtype), jax.ShapeDtypeStruct(state0.shape, F32)],
        scratch_shapes=[pltpu.VMEM((dk, dk), F32)],
        compiler_params=_cparams(("parallel", "parallel", "arbitrary")),
        name="retention",
    )(q, k, v, gate, state0)


def _mem_kv_kernel(mem_ref, g_ref, w_ref, o_ref):
    xn = _rms(mem_ref[...], g_ref[...]).astype(BF16)
    o_ref[...] = jnp.dot(xn, w_ref[...], preferred_element_type=F32)


def _mem_kv(mem, g, w):
    m, dm = mem.shape
    n = w.shape[1]
    return pl.pallas_call(
        _mem_kv_kernel,
        grid=(n // dm,),
        in_specs=[pl.BlockSpec((m, dm), lambda i: (0, 0)), pl.BlockSpec((1, dm), lambda i: (0, 0)),
                  pl.BlockSpec((dm, dm), lambda i: (0, i))],
        out_specs=pl.BlockSpec((m, dm), lambda i: (0, i)),
        out_shape=jax.ShapeDtypeStruct((m, n), F32),
        compiler_params=_cparams(("parallel",)),
        name="mem_kv",
    )(mem, g, w)


def _mem_attn_kernel(q_ref, mk_ref, mv_ref, o_ref):
    q = q_ref[0].astype(BF16)
    mk = mk_ref[0].astype(BF16)
    mv = mv_ref[0].astype(BF16)
    d = q.shape[-1] // H_MEM
    outs = []
    for h in range(H_MEM):
        sl = slice(h * d, (h + 1) * d)
        s = lax.dot_general(q[:, sl], mk[:, sl], (((1,), (1,)), ((), ())), preferred_element_type=F32)
        m = jnp.max(s, axis=-1, keepdims=True)
        p = jnp.exp(s - m)
        p = p / jnp.sum(p, axis=-1, keepdims=True)
        outs.append(jnp.dot(p.astype(BF16), mv[:, sl], preferred_element_type=F32))
    o_ref[0] = jnp.concatenate(outs, axis=-1).astype(o_ref.dtype)


def _mem_attn(q, mk, mv, y_dtype):
    b, t, dm = q.shape
    tm = _row_tile(t, 512)
    qb = pl.BlockSpec((1, tm, dm), lambda bi, i: (bi, i, 0))
    mb = pl.BlockSpec((1,) + mk.shape[1:], lambda bi, i: (bi, 0, 0))
    return pl.pallas_call(
        _mem_attn_kernel,
        grid=(b, t // tm),
        in_specs=[qb, mb, mb],
        out_specs=qb,
        out_shape=jax.ShapeDtypeStruct((b, t, dm), y_dtype),
        compiler_params=_cparams(("parallel", "parallel")),
        name="mem_attn",
    )(q, mk, mv)


def _merge_kernel(x_ref, yd_ref, yr_ref, ym_ref, gt_ref, wb_ref, wo_ref, g2_ref, rw_ref, rb_ref,
                  h1_ref, hn_ref, lg_ref):
    dm = x_ref.shape[-1]
    mixed = jnp.zeros(x_ref.shape, F32)
    for i, y_ref in enumerate((yd_ref, yr_ref, ym_ref)):
        proj = jnp.dot(y_ref[...].astype(BF16), wb_ref[i], preferred_element_type=F32)
        mixed = mixed + jax.nn.sigmoid(gt_ref[:, i * dm:(i + 1) * dm]) * proj
    h1 = x_ref[...] + jnp.dot(mixed.astype(BF16), wo_ref[...], preferred_element_type=F32)
    h1_ref[...] = h1
    hn = _rms(h1, g2_ref[...])
    hn_ref[...] = hn
    lg_ref[...] = jnp.dot(hn.astype(BF16), rw_ref[...], preferred_element_type=F32) + rb_ref[...]


def _merge(x, yd, yr, ym, gt, wb, wo, g2, rw, rb):
    t, dm = x.shape
    tm = _row_tile(t, 512)
    row = lambda i: (i, 0)
    blk = pl.BlockSpec((tm, dm), row)
    c2 = lambda i: (0, 0)
    return pl.pallas_call(
        _merge_kernel,
        grid=(t // tm,),
        in_specs=[blk, blk, blk, blk, pl.BlockSpec((tm, N_BRANCH * dm), row),
                  pl.BlockSpec(wb.shape, lambda i: (0, 0, 0)), pl.BlockSpec(wo.shape, c2),
                  pl.BlockSpec((1, dm), c2), pl.BlockSpec(rw.shape, c2), pl.BlockSpec(rb.shape, c2)],
        out_specs=[blk, blk, pl.BlockSpec((tm, V7X_LANES), row)],
        out_shape=[jax.ShapeDtypeStruct((t, dm), F32), jax.ShapeDtypeStruct((t, dm), F32),
                   jax.ShapeDtypeStruct((t, V7X_LANES), F32)],
        compiler_params=_cparams(("parallel",)),
        name="merge",
    )(x, yd, yr, ym, gt, wb, wo, g2, rw, rb)


def _route_kernel(lg_ref, idx_ref, gate_ref, rank_ref, cnt_ref, carry_s):
    i = pl.program_id(0)

    @pl.when(i == 0)
    def _():
        carry_s[...] = jnp.zeros(carry_s.shape, F32)

    l = lg_ref[...]
    tm = l.shape[0]
    lane = lax.broadcasted_iota(I32, l.shape, 1)
    vals, idxs, hots = [], [], []
    for _ in range(TOP_K):
        m = jnp.max(l, axis=-1, keepdims=True)
        ik = jnp.min(jnp.where(l == m, lane, V7X_LANES), axis=-1, keepdims=True)
        hot = lane == ik
        vals.append(m)
        idxs.append(ik)
        hots.append(hot)
        l = jnp.where(hot, NEG_INF, l)
    es = [jnp.exp(v - vals[0]) for v in vals]
    den = es[0] + es[1] + es[2] + es[3]
    picked = jnp.zeros(l.shape, F32)
    for hot in hots:
        picked = picked + jnp.where(hot, 1.0, 0.0)
    r = lax.broadcasted_iota(I32, (tm, tm), 0)
    c = lax.broadcasted_iota(I32, (tm, tm), 1)
    before = jnp.where(c < r, 1.0, 0.0).astype(BF16)
    cum = jnp.dot(before, picked.astype(BF16), preferred_element_type=F32) + carry_s[0:1, :]
    idx_o = jnp.zeros(l.shape, I32)
    gate_o = jnp.zeros(l.shape, F32)
    rank_o = jnp.zeros(l.shape, I32)
    for k in range(TOP_K):
        rk = jnp.sum(jnp.where(hots[k], cum, 0.0), axis=-1, keepdims=True).astype(I32)
        idx_o = jnp.where(lane == k, idxs[k], idx_o)
        gate_o = jnp.where(lane == k, es[k] / den, gate_o)
        rank_o = jnp.where(lane == k, rk, rank_o)
    idx_ref[...] = idx_o
    gate_ref[...] = gate_o
    rank_ref[...] = rank_o
    carry_s[...] = carry_s[...] + jnp.sum(picked, axis=0, keepdims=True)
    cnt_ref[...] = carry_s[...]


def _route(logits):
    t = logits.shape[0]
    tm = _row_tile(t, 256)
    row = lambda i: (i, 0)
    blk = pl.BlockSpec((tm, V7X_LANES), row)
    return pl.pallas_call(
        _route_kernel,
        grid=(t // tm,),
        in_specs=[blk],
        out_specs=[blk, blk, blk, pl.BlockSpec((8, V7X_LANES), lambda i: (0, 0))],
        out_shape=[jax.ShapeDtypeStruct((t, V7X_LANES), I32), jax.ShapeDtypeStruct((t, V7X_LANES), F32),
                   jax.ShapeDtypeStruct((t, V7X_LANES), I32), jax.ShapeDtypeStruct((8, V7X_LANES), F32)],
        scratch_shapes=[pltpu.VMEM((8, V7X_LANES), F32)],
        compiler_params=_cparams(("arbitrary",)),
        name="route",
    )(logits)


def _row_copy(src, src_row, dst, dst_row, sem):
    return pltpu.make_async_copy(src.at[pl.ds(src_row, 1)], dst.at[pl.ds(dst_row, 1)], sem)


def _dispatch_kernel(dest_ref, hn_ref, xs_in_ref, xs_ref, sem):
    del xs_in_ref
    tm = hn_ref.shape[0]

    def issue(r, c):
        for k in range(TOP_K):
            _row_copy(hn_ref, r, xs_ref, dest_ref[r * TOP_K + k], sem).start()
        return c

    lax.fori_loop(0, tm, issue, 0)

    def drain(r, c):
        for k in range(TOP_K):
            _row_copy(hn_ref, r, xs_ref, dest_ref[r * TOP_K + k], sem).wait()
        return c

    lax.fori_loop(0, tm, drain, 0)


def _dispatch(dest_flat, hn, xs):
    t, dm = hn.shape
    tm = _row_tile(t, 128)
    return pl.pallas_call(
        _dispatch_kernel,
        grid=(t // tm,),
        in_specs=[pl.BlockSpec((tm * TOP_K,), lambda i: (i,), memory_space=pltpu.SMEM),
                  pl.BlockSpec((tm, dm), lambda i: (i, 0)),
                  pl.BlockSpec(memory_space=pl.ANY)],
        out_specs=pl.BlockSpec(memory_space=pl.ANY),
        out_shape=jax.ShapeDtypeStruct(xs.shape, xs.dtype),
        scratch_shapes=[pltpu.SemaphoreType.DMA(())],
        input_output_aliases={2: 0},
        compiler_params=_cparams(("arbitrary",)),
        name="moe_dispatch",
    )(dest_flat, hn, xs)


def _expert_kernel(be_ref, nu_ref, xs_ref, w1g_ref, w1l_ref, b1g_ref, b1l_ref, w2_ref, b2_ref, o_ref):
    i = pl.program_id(0)

    @pl.when(i < nu_ref[0])
    def _():
        x = xs_ref[...].astype(BF16)
        ug = jnp.dot(x, w1g_ref[0], preferred_element_type=F32) + b1g_ref[0]
        ul = jnp.dot(x, w1l_ref[0], preferred_element_type=F32) + b1l_ref[0]
        x_glu = jnp.minimum(ug, SWIGLU_LIMIT)
        x_lin = jnp.clip(ul, -SWIGLU_LIMIT, SWIGLU_LIMIT)
        act = x_glu * jax.nn.sigmoid(SWIGLU_ALPHA * x_glu) * (x_lin + 1.0)
        o_ref[...] = jnp.dot(act.astype(BF16), w2_ref[0], preferred_element_type=F32) + b2_ref[0]

    @pl.when(i >= nu_ref[0])
    def _():
        o_ref[...] = jnp.zeros(o_ref.shape, F32)


def _experts(block_exp, n_used, xs, w1g, w1l, b1g, b1l, w2, b2):
    cap, dm = xs.shape
    dff = w1g.shape[-1]
    n_blocks = cap // MOE_BLOCK
    row = lambda i, be, nu: (i, 0)
    wsel = lambda i, be, nu: (be[i], 0, 0)
    grid_spec = pltpu.PrefetchScalarGridSpec(
        num_scalar_prefetch=2,
        grid=(n_blocks,),
        in_specs=[pl.BlockSpec((MOE_BLOCK, dm), row),
                  pl.BlockSpec((1, dm, dff), wsel), pl.BlockSpec((1, dm, dff), wsel),
                  pl.BlockSpec((1, 1, dff), wsel), pl.BlockSpec((1, 1, dff), wsel),
                  pl.BlockSpec((1, dff, dm), wsel), pl.BlockSpec((1, 1, dm), wsel)],
        out_specs=pl.BlockSpec((MOE_BLOCK, dm), row),
    )
    return pl.pallas_call(
        _expert_kernel,
        grid_spec=grid_spec,
        out_shape=jax.ShapeDtypeStruct((cap, dm), F32),
        compiler_params=_cparams(("arbitrary",)),
        name="moe_experts",
    )(block_exp, n_used, xs, w1g, w1l, b1g, b1l, w2, b2)


def _combine_kernel(dest_ref, gate_ref, h1_ref, gf_ref, eo_ref, y_ref, buf, sem):
    tm = h1_ref.shape[0]

    def issue(r, c):
        for k in range(TOP_K):
            _row_copy(eo_ref, dest_ref[r * TOP_K + k], buf.at[k], r, sem).start()
        return c

    lax.fori_loop(0, tm, issue, 0)

    def drain(r, c):
        for k in range(TOP_K):
            _row_copy(eo_ref, dest_ref[r * TOP_K + k], buf.at[k], r, sem).wait()
        return c

    lax.fori_loop(0, tm, drain, 0)
    gate = gate_ref[...]
    y = h1_ref[...]
    for k in range(TOP_K):
        y = y + gate[:, k:k + 1] * buf[k]
    y_ref[...] = _rms(y, gf_ref[...])


def _combine(dest_flat, gate, h1, gf, eo):
    t, dm = h1.shape
    tm = _row_tile(t, 128)
    row = lambda i: (i, 0)
    return pl.pallas_call(
        _combine_kernel,
        grid=(t // tm,),
        in_specs=[pl.BlockSpec((tm * TOP_K,), lambda i: (i,), memory_space=pltpu.SMEM),
                  pl.BlockSpec((tm, V7X_LANES), row), pl.BlockSpec((tm, dm), row),
                  pl.BlockSpec((1, dm), lambda i: (0, 0)), pl.BlockSpec(memory_space=pl.ANY)],
        out_specs=pl.BlockSpec((tm, dm), row),
        out_shape=jax.ShapeDtypeStruct((t, dm), F32),
        scratch_shapes=[pltpu.VMEM((TOP_K, tm, dm), F32), pltpu.SemaphoreType.DMA(())],
        compiler_params=_cparams(("arbitrary",)),
        name="moe_combine",
    )(dest_flat, gate, h1, gf, eo)


def _moe_and_final_norm(parts, router_b_unused, w1, b1, w2, b2, normf_g):
    del router_b_unused
    sizes = [p[0].shape[0] for p in parts]
    t = sum(sizes)
    dm = parts[0][0].shape[1]
    logits = jnp.concatenate([p[2] for p in parts], axis=0)
    idx, gate, rank, counts = _route(logits)
    cnt = counts[0, :N_EXPERTS].astype(I32)
    padded = ((cnt + MOE_BLOCK - 1) // MOE_BLOCK) * MOE_BLOCK
    pad_ends = jnp.cumsum(padded)
    pad_starts = pad_ends - padded
    n_blocks = -(-(t * TOP_K + N_EXPERTS * (MOE_BLOCK - 1)) // MOE_BLOCK)
    cap = n_blocks * MOE_BLOCK
    block_exp = jnp.minimum(jnp.searchsorted(pad_ends, jnp.arange(n_blocks, dtype=I32) * MOE_BLOCK, side="right"),
                            N_EXPERTS - 1).astype(I32)
    n_used = (pad_ends[-1:] // MOE_BLOCK).astype(I32)
    dest = (pad_starts[idx[:, :TOP_K]] + rank[:, :TOP_K]).astype(I32).reshape(-1)

    xs = jnp.zeros((cap, dm), F32)
    off = 0
    for (h1, hn, _), n in zip(parts, sizes):
        xs = _dispatch(dest[off * TOP_K:(off + n) * TOP_K], hn, xs)
        off += n
    w1g = w1[:, :, 0::2].astype(BF16)
    w1l = w1[:, :, 1::2].astype(BF16)
    b1g = b1[:, None, 0::2]
    b1l = b1[:, None, 1::2]
    eo = _experts(block_exp, n_used, xs, w1g, w1l, b1g, b1l, w2.astype(BF16), b2[:, None, :])
    outs = []
    off = 0
    for (h1, hn, _), n in zip(parts, sizes):
        outs.append(_combine(dest[off * TOP_K:(off + n) * TOP_K], gate[off:off + n], h1, normf_g, eo))
        off += n
    return outs


def _rope_tables(pos, half):
    inv = ROPE_BASE ** (-jnp.arange(half, dtype=F32) / half)
    ang = pos.astype(F32)[:, None] * inv[None, :]
    return jnp.cos(ang), jnp.sin(ang)


def kernel(x_prompt, x_sample, cache_k, cache_v, state_ret, cache_mem_k, cache_mem_v, page_table, mem_prompt,
           norm1_g, w_in, lambda_q1, lambda_k1, lambda_q2, lambda_k2, subln_g, norm_mem_g, w_mem_kv, w_branch,
           w_out, norm2_g, router_w, router_b, w1, b1, w2, b2, normf_g):
    b, s, dm = x_prompt.shape
    db, t, _ = x_sample.shape
    depth = w_in.shape[0]
    assert depth == 1, "final norm is fused into the last layer's MoE combine; one layer supported"
    assert b == 1
    page = cache_k.shape[2]
    past_len = page_table.shape[1] * page
    dk_ret = dm // H_RET
    n_mem = mem_prompt.shape[1]

    cos_p, sin_p = _rope_tables(jnp.arange(s), dk_ret // 2)
    cos_s, sin_s = _rope_tables(past_len + jnp.arange(t), dk_ret // 2)
    cos_s = jnp.tile(cos_s, (db, 1))
    sin_s = jnp.tile(sin_s, (db, 1))

    l = 0
    lam_init = 0.8 - 0.6 * math.exp(-0.3 * l)
    lam4 = jnp.stack([lambda_q1[l], lambda_k1[l], lambda_q2[l], lambda_k2[l]]).astype(F32)
    g1 = norm1_g[l][None, :]
    sub_g = subln_g[l][None, :]
    d3 = 3 * dm
    w_l = w_in[l]
    w_diff = w_l[:, :d3].astype(BF16)
    w_ret = w_l[:, d3:d3 + 4 * dm].astype(BF16)
    w_mg = w_l[:, d3 + 4 * dm:].astype(BF16)
    wb = w_branch[l].astype(BF16)
    wo = w_out[l].astype(BF16)
    g2 = norm2_g[l][None, :]
    rw = jnp.zeros((dm, V7X_LANES), F32).at[:, :N_EXPERTS].set(router_w[l]).astype(BF16)
    rb = jnp.full((1, V7X_LANES), NEG_INF, F32).at[0, :N_EXPERTS].set(router_b[l])

    xp = x_prompt.reshape(s, dm)
    q_d, k_f, v_f, k_b, v_b = _proj_diff(xp, g1, w_diff, BF16)
    y_d = _diff_attn_prompt(q_d, k_b, v_b, lam4, sub_g, lam_init)
    rq, rk, rv, rg = _proj_ret(xp, g1, w_ret, cos_p, sin_p)
    to3 = lambda a: a.reshape(1, s, dm)
    y_r, st_p = _retention(to3(rq), to3(rk), to3(rv), to3(rg), jnp.zeros((1, H_RET, dk_ret, dk_ret), F32),
                           chunk=128, y_dtype=BF16, mm_dtype=BF16)
    mq, gt = _proj_mem_gate(xp, g1, w_mg, BF16)
    mkv = _mem_kv(mem_prompt.reshape(n_mem, dm), norm_mem_g[l][None, :], w_mem_kv[l].astype(BF16))
    mk, mv = mkv[:, :dm], mkv[:, dm:]
    y_m = _mem_attn(mq[None], mk[None], mv[None], BF16)
    part_p = _merge(xp, y_d, y_r.reshape(s, dm), y_m.reshape(s, dm), gt, wb, wo, g2, rw, rb)

    ns = db * t
    xs_ = x_sample.reshape(ns, dm)
    q_s, ks_f, vs_f, _, _ = _proj_diff(xs_, g1, w_diff, F32)
    k_s5 = ks_f.reshape(db, t, H_DIFF, D_V)
    v_s5 = vs_f.reshape(db, t, H_DIFF, D_V)
    y_ds = _diff_attn_sample(page_table, q_s.reshape(db, t, dm), k_s5, v_s5, cache_k[l], cache_v[l],
                             lam4, sub_g, lam_init)
    rq, rk, rv, rg = _proj_ret(xs_, g1, w_ret, cos_s, sin_s)
    tos = lambda a: a.reshape(db, t, dm)
    y_rs, st_s = _retention(tos(rq), tos(rk), tos(rv), tos(rg), state_ret[l].astype(F32),
                            chunk=t, y_dtype=F32, mm_dtype=F32)
    mq_s, gt_s = _proj_mem_gate(xs_, g1, w_mg, F32)
    y_ms = _mem_attn(mq_s.reshape(db, t, dm), cache_mem_k[l].reshape(db, n_mem, dm),
                     cache_mem_v[l].reshape(db, n_mem, dm), F32)
    part_s = _merge(xs_, y_ds.reshape(ns, dm), y_rs.reshape(ns, dm), y_ms.reshape(ns, dm), gt_s, wb, wo, g2, rw, rb)

    y_p, y_s = _moe_and_final_norm([part_p, part_s], None, w1[l], b1[l], w2[l], b2[l], normf_g[None, :])

    return (y_p.reshape(b, s, dm), y_s.reshape(db, t, dm),
            k_f.reshape(1, b, s, H_DIFF, D_V), v_f.reshape(1, b, s, H_DIFF, D_V),
            st_p.reshape(1, b, H_RET, dk_ret, dk_ret),
            mk.reshape(1, b, n_mem, H_MEM, dm // H_MEM), mv.reshape(1, b, n_mem, H_MEM, dm // H_MEM),
            k_s5[None], v_s5[None], st_s[None].astype(state_ret.dtype))
```

```python
import functools
import math

import jax
import jax.numpy as jnp
import numpy as np
from jax import lax
from jax.experimental import pallas as pl
from jax.experimental.pallas import tpu as pltpu

F32 = jnp.float32
BF16 = jnp.bfloat16
I32 = jnp.int32

H_DIFF = 8
D_QK = 64
D_V = 128
H_RET = 4
H_MEM = 4
N_BRANCH = 3
N_EXPERTS = 32
TOP_K = 4
SWIGLU_LIMIT = 7.0
SWIGLU_ALPHA = 1.702
ROPE_BASE = 10000.0
RMS_EPS = 1e-6

V7X_LANES = 128
V7X_VMEM_LIMIT_BYTES = 56 * 1024 * 1024

MOE_BLOCK = 256
RET_CHUNK = 256
NEG_INF = float("-inf")
DIFF_Q_SCALE = (D_QK ** -0.5) * math.log2(math.e)


def _cparams(sem):
    return pltpu.CompilerParams(dimension_semantics=sem, vmem_limit_bytes=V7X_VMEM_LIMIT_BYTES)


def _rms(x, g=None):
    y = x * lax.rsqrt(jnp.mean(x * x, axis=-1, keepdims=True) + RMS_EPS)
    return y if g is None else y * g


def _row_tile(n, pref):
    t = min(n, pref)
    while n % t:
        t //= 2
    return t


def _proj_diff_kernel(x_ref, g_ref, w_ref, q_ref, kf_ref, vf_ref, kb_ref, vt_ref):
    xn = _rms(x_ref[...], g_ref[...]).astype(BF16)
    d = kf_ref.shape[-1]
    q = jnp.dot(xn, w_ref[:, 0:d], preferred_element_type=F32)
    q_ref[...] = (q * DIFF_Q_SCALE).astype(q_ref.dtype)
    k = jnp.dot(xn, w_ref[:, d:2 * d], preferred_element_type=F32)
    kf_ref[...] = k
    kb_ref[...] = k.astype(BF16)
    v = jnp.dot(xn, w_ref[:, 2 * d:3 * d], preferred_element_type=F32)
    vf_ref[...] = v
    vt_ref[...] = v.T.astype(BF16)


def _proj_diff(x, g, w, q_dtype):
    t, dm = x.shape
    tm = _row_tile(t, 512)
    row = lambda i: (i, 0)
    blk = pl.BlockSpec((tm, dm), row)
    return pl.pallas_call(
        _proj_diff_kernel,
        grid=(t // tm,),
        in_specs=[blk, pl.BlockSpec((1, dm), lambda i: (0, 0)), pl.BlockSpec(w.shape, lambda i: (0, 0))],
        out_specs=[blk] * 4 + [pl.BlockSpec((dm, tm), lambda i: (0, i))],
        out_shape=[jax.ShapeDtypeStruct((t, dm), q_dtype), jax.ShapeDtypeStruct((t, dm), F32),
                   jax.ShapeDtypeStruct((t, dm), F32), jax.ShapeDtypeStruct((t, dm), BF16),
                   jax.ShapeDtypeStruct((dm, t), BF16)],
        compiler_params=_cparams(("parallel",)),
        name="proj_diff",
    )(x, g, w)


def _proj_ret_kernel(x_ref, g_ref, w_ref, cos_ref, sin_ref, q_ref, k_ref, v_ref, gate_ref):
    xn = _rms(x_ref[...], g_ref[...]).astype(BF16)
    d = q_ref.shape[-1]
    dk = d // H_RET
    half = dk // 2
    cos = cos_ref[...]
    sin = sin_ref[...]

    def rope(u, scale):
        outs = []
        for h in range(H_RET):
            x1 = u[:, h * dk:h * dk + half]
            x2 = u[:, h * dk + half:(h + 1) * dk]
            outs.append((x1 * cos - x2 * sin) * scale)
            outs.append((x1 * sin + x2 * cos) * scale)
        return jnp.concatenate(outs, axis=-1)

    q_ref[...] = rope(jnp.dot(xn, w_ref[:, 0:d], preferred_element_type=F32), 1.0)
    k_ref[...] = rope(jnp.dot(xn, w_ref[:, d:2 * d], preferred_element_type=F32), dk ** -0.5)
    v_ref[...] = jnp.dot(xn, w_ref[:, 2 * d:3 * d], preferred_element_type=F32)
    gate_ref[...] = jnp.dot(xn, w_ref[:, 3 * d:4 * d], preferred_element_type=F32)


def _proj_ret(x, g, w, cos, sin):
    t, dm = x.shape
    tm = _row_tile(t, 512)
    row = lambda i: (i, 0)
    blk = pl.BlockSpec((tm, dm), row)
    tab = pl.BlockSpec((tm, cos.shape[1]), row)
    return pl.pallas_call(
        _proj_ret_kernel,
        grid=(t // tm,),
        in_specs=[blk, pl.BlockSpec((1, dm), lambda i: (0, 0)), pl.BlockSpec(w.shape, lambda i: (0, 0)), tab, tab],
        out_specs=[blk] * 4,
        out_shape=[jax.ShapeDtypeStruct((t, dm), F32)] * 4,
        compiler_params=_cparams(("parallel",)),
        name="proj_ret",
    )(x, g, w, cos, sin)


def _proj_mem_gate_kernel(x_ref, g_ref, w_ref, mq_ref, gt_ref):
    xn = _rms(x_ref[...], g_ref[...]).astype(BF16)
    d = mq_ref.shape[-1]
    mq = jnp.dot(xn, w_ref[:, 0:d], preferred_element_type=F32)
    mq_ref[...] = (mq * ((d // H_MEM) ** -0.5)).astype(mq_ref.dtype)
    gt_ref[...] = jnp.dot(xn, w_ref[:, d:], preferred_element_type=F32)


def _proj_mem_gate(x, g, w, mq_dtype):
    t, dm = x.shape
    tm = _row_tile(t, 512)
    row = lambda i: (i, 0)
    return pl.pallas_call(
        _proj_mem_gate_kernel,
        grid=(t // tm,),
        in_specs=[pl.BlockSpec((tm, dm), row), pl.BlockSpec((1, dm), lambda i: (0, 0)),
                  pl.BlockSpec(w.shape, lambda i: (0, 0))],
        out_specs=[pl.BlockSpec((tm, dm), row), pl.BlockSpec((tm, N_BRANCH * dm), row)],
        out_shape=[jax.ShapeDtypeStruct((t, dm), mq_dtype), jax.ShapeDtypeStruct((t, N_BRANCH * dm), F32)],
        compiler_params=_cparams(("parallel",)),
        name="proj_mem_gate",
    )(x, g, w)


def _lambda_value(lam_ref, lam_init):
    t = lam_ref[...]
    a = jnp.sum(t[0:1] * t[1:2], axis=-1, keepdims=True)
    b = jnp.sum(t[2:3] * t[3:4], axis=-1, keepdims=True)
    return jnp.exp(a) - jnp.exp(b) + lam_init


def _diff_attn_prompt_kernel(q_ref, k_ref, vt_ref, lam_ref, g_ref, o_ref, m_s, l_s, acc_s, sa_s, sb_s, mba_s, mbb_s,
                             *, tq, tk, rc, lam_init):
    qi = pl.program_id(1)
    q = q_ref[...]
    lane = lax.broadcasted_iota(I32, q.shape, 1)
    zero = jnp.zeros_like(q)
    qz = jnp.concatenate([jnp.where(lane < D_QK, q, zero), jnp.where(lane >= D_QK, q, zero)], axis=0)
    m_s[...] = jnp.full(m_s.shape, NEG_INF, F32)
    l_s[...] = jnp.zeros(l_s.shape, F32)
    acc_s[...] = jnp.zeros(acc_s.shape, F32)
    n_full = (qi * tq) // tk

    def scores(j, masked, s_buf, mb_buf):
        ks = k_ref[pl.ds(pl.multiple_of(j * tk, tk), tk), :]
        s = lax.dot_general(ks, qz, (((1,), (1,)), ((), ())), preferred_element_type=F32)
        if masked:
            kpos = lax.broadcasted_iota(I32, s.shape, 0) + j * tk
            col = lax.broadcasted_iota(I32, s.shape, 1)
            qpos = jnp.where(col >= tq, col - tq, col) + qi * tq
            s = jnp.where(kpos <= qpos, s, NEG_INF)
        s_buf[...] = s
        mb_buf[...] = jnp.max(s, axis=0, keepdims=True)

    def absorb(s_buf, mb_buf, vblock):
        m_old = m_s[...]
        m_new = jnp.maximum(m_old, mb_buf[...])
        alpha = jnp.exp2(m_old - m_new)
        part = jnp.zeros((8, 2 * tq), F32)
        vstart = pl.multiple_of(vblock * tk, tk)
        pv = None
        for r in range(tk // rc):
            p = jnp.exp2(s_buf[r * rc:(r + 1) * rc, :] - m_new)
            part = part + jnp.sum(p.reshape(rc // 8, 8, 2 * tq), axis=0)
            vt = vt_ref[:, pl.ds(vstart + r * rc, rc)]
            d = jnp.dot(vt, p.astype(BF16), preferred_element_type=F32)
            pv = d if pv is None else pv + d
        l_s[...] = alpha * l_s[...] + jnp.sum(part, axis=0, keepdims=True)
        acc_s[...] = alpha * acc_s[...] + pv
        m_s[...] = m_new

    scores(n_full, True, sa_s, mba_s)
    n_pairs = n_full // 2

    def body(i, c):
        scores(2 * i, False, sb_s, mbb_s)
        absorb(sa_s, mba_s, jnp.where(i == 0, n_full, 2 * i - 1))
        scores(2 * i + 1, False, sa_s, mba_s)
        absorb(sb_s, mbb_s, 2 * i)
        return c

    lax.fori_loop(0, n_pairs, body, 0)
    pending = jnp.where(n_pairs == 0, n_full, 2 * n_pairs - 1)

    @pl.when(n_full % 2 == 1)
    def _():
        scores(n_full - 1, False, sb_s, mbb_s)
        absorb(sa_s, mba_s, pending)
        absorb(sb_s, mbb_s, n_full - 1)

    @pl.when(n_full % 2 == 0)
    def _():
        absorb(sa_s, mba_s, pending)

    o = acc_s[...] / l_s[...]
    lam = _lambda_value(lam_ref, lam_init)
    od = o[:, :tq] - lam * o[:, tq:]
    y = od * lax.rsqrt(jnp.mean(od * od, axis=0, keepdims=True) + RMS_EPS) * g_ref[...] * (1.0 - lam_init)
    o_ref[...] = y.T.astype(o_ref.dtype)


def _diff_attn_prompt(q, k, vt, lam4, subln_g_col, lam_init):
    s, dm = q.shape
    tq = _row_tile(s, 256)
    tk = _row_tile(s, 1024)
    rc = _row_tile(tk, 256)
    kern = functools.partial(_diff_attn_prompt_kernel, tq=tq, tk=tk, rc=rc, lam_init=lam_init)
    row_stat = pltpu.VMEM((1, 2 * tq), F32)
    score_buf = pltpu.VMEM((tk, 2 * tq), F32)
    return pl.pallas_call(
        kern,
        grid=(H_DIFF, s // tq),
        in_specs=[pl.BlockSpec((tq, D_V), lambda h, i: (i, h)),
                  pl.BlockSpec((s, D_V), lambda h, i: (0, h)),
                  pl.BlockSpec((D_V, s), lambda h, i: (h, 0)),
                  pl.BlockSpec(lam4.shape, lambda h, i: (0, 0)),
                  pl.BlockSpec((D_V, 1), lambda h, i: (0, 0))],
        out_specs=pl.BlockSpec((tq, D_V), lambda h, i: (i, h)),
        out_shape=jax.ShapeDtypeStruct((s, dm), BF16),
        scratch_shapes=[row_stat, row_stat, pltpu.VMEM((D_V, 2 * tq), F32),
                        score_buf, score_buf, row_stat, row_stat],
        compiler_params=_cparams(("parallel", "parallel")),
        name="diff_attn_prompt",
    )(q, k, vt, lam4, subln_g_col)


def _diff_attn_sample_kernel(pt_ref, q_ref, kn_ref, vn_ref, lam_ref, g_ref, *rest, pps, lam_init):
    k_refs = rest[:pps]
    v_refs = rest[pps:2 * pps]
    o_ref, qmat_s, bias_s, m_s, l_s, acc_s, s_s = rest[2 * pps:]
    j = pl.program_id(1)
    nq = q_ref.shape[1]
    pc = bias_s.shape[1]

    @pl.when(j == 0)
    def _():
        qb = q_ref[0]
        lane = lax.broadcasted_iota(I32, (nq, D_V), 1)
        parts = []
        for h in range(H_DIFF):
            blk = qb[:, h * D_V:(h + 1) * D_V]
            parts.append(jnp.where(lane < D_QK, blk, 0.0))
            parts.append(jnp.where(lane >= D_QK, blk, 0.0))
        qmat_s[...] = jnp.concatenate(parts, axis=0).astype(BF16)
        r = lax.broadcasted_iota(I32, bias_s.shape, 0)
        c = lax.broadcasted_iota(I32, bias_s.shape, 1)
        bias_s[...] = jnp.where((c % H_DIFF) == (r // (2 * nq)), 0.0, NEG_INF)
        m_s[...] = jnp.full(m_s.shape, NEG_INF, F32)
        l_s[...] = jnp.zeros(l_s.shape, F32)
        acc_s[...] = jnp.zeros(acc_s.shape, F32)

    def flat(page):
        return page.reshape(page.shape[0] * page.shape[1], page.shape[2]).astype(BF16)

    qmat = qmat_s[...]
    mb = None
    for r in range(pps):
        s = lax.dot_general(qmat, flat(k_refs[r][0, 0]), (((1,), (1,)), ((), ())),
                            preferred_element_type=F32) + bias_s[...]
        s_s[:, r * pc:(r + 1) * pc] = s
        mr = jnp.max(s, axis=-1, keepdims=True)
        mb = mr if mb is None else jnp.maximum(mb, mr)
    m_old = m_s[...]
    m_new = jnp.maximum(m_old, mb)
    alpha = jnp.exp2(m_old - m_new)
    lsum = None
    pv = None
    for r in range(pps):
        p = jnp.exp2(s_s[:, r * pc:(r + 1) * pc] - m_new)
        ls = jnp.sum(p, axis=-1, keepdims=True)
        lsum = ls if lsum is None else lsum + ls
        d = jnp.dot(p.astype(BF16), flat(v_refs[r][0, 0]), preferred_element_type=F32)
        pv = d if pv is None else pv + d
    l_s[...] = alpha * l_s[...] + lsum
    acc_s[...] = alpha * acc_s[...] + pv
    m_s[...] = m_new

    @pl.when(j == pl.num_programs(1) - 1)
    def _():
        s = lax.dot_general(qmat, flat(kn_ref[0]), (((1,), (1,)), ((), ())), preferred_element_type=F32)
        r = lax.broadcasted_iota(I32, s.shape, 0)
        c = lax.broadcasted_iota(I32, s.shape, 1)
        keep = ((c % H_DIFF) == (r // (2 * nq))) & ((c // H_DIFF) <= (r % nq))
        s = jnp.where(keep, s, NEG_INF)
        m_o = m_s[...]
        m_n = jnp.maximum(m_o, jnp.max(s, axis=-1, keepdims=True))
        al = jnp.exp2(m_o - m_n)
        p = jnp.exp2(s - m_n)
        l = al * l_s[...] + jnp.sum(p, axis=-1, keepdims=True)
        acc = al * acc_s[...] + jnp.dot(p.astype(BF16), flat(vn_ref[0]), preferred_element_type=F32)
        o = acc / l
        lam = _lambda_value(lam_ref, lam_init)
        g = g_ref[...]
        outs = []
        for h in range(H_DIFF):
            base = h * 2 * nq
            od = o[base:base + nq] - lam * o[base + nq:base + 2 * nq]
            outs.append(_rms(od, g) * (1.0 - lam_init))
        o_ref[0] = jnp.concatenate(outs, axis=-1)


def _diff_attn_sample(page_table, q, k_new, v_new, cache_k, cache_v, layer, lam4, subln_g, lam_init, pps=4):
    b, nq, dm = q.shape
    n_pages = page_table.shape[1]
    while n_pages % pps:
        pps //= 2
    page_shape = (1, 1) + cache_k.shape[2:]
    page_specs = [pl.BlockSpec(page_shape, functools.partial(
        lambda bi, j, pt, r: (layer, pt[bi, j * pps + r], 0, 0, 0), r=r)) for r in range(pps)]
    rows = H_DIFF * 2 * nq
    cols = cache_k.shape[2] * cache_k.shape[3]
    kern = functools.partial(_diff_attn_sample_kernel, pps=pps, lam_init=lam_init)
    per_seq3 = lambda bi, j, pt: (bi, 0, 0)
    per_seq4 = lambda bi, j, pt: (bi, 0, 0, 0)
    const2 = lambda bi, j, pt: (0, 0)
    grid_spec = pltpu.PrefetchScalarGridSpec(
        num_scalar_prefetch=1,
        grid=(b, n_pages // pps),
        in_specs=[pl.BlockSpec((1, nq, dm), per_seq3),
                  pl.BlockSpec((1,) + k_new.shape[1:], per_seq4),
                  pl.BlockSpec((1,) + v_new.shape[1:], per_seq4),
                  pl.BlockSpec(lam4.shape, const2),
                  pl.BlockSpec((1, D_V), const2)] + page_specs + page_specs,
        out_specs=pl.BlockSpec((1, nq, dm), per_seq3),
        scratch_shapes=[pltpu.VMEM((rows, D_V), BF16), pltpu.VMEM((rows, cols), F32),
                        pltpu.VMEM((rows, 1), F32), pltpu.VMEM((rows, 1), F32), pltpu.VMEM((rows, D_V), F32),
                        pltpu.VMEM((rows, pps * cols), F32)],
    )
    return pl.pallas_call(
        kern,
        grid_spec=grid_spec,
        out_shape=jax.ShapeDtypeStruct((b, nq, dm), F32),
        compiler_params=_cparams(("parallel", "arbitrary")),
        name="diff_attn_sample",
    )(page_table, q, k_new, v_new, lam4, subln_g, *([cache_k] * pps), *([cache_v] * pps))


def _retention_log_decay(h):
    return float(np.log(np.float32(1.0) - np.exp2(np.float32(-5.0 - h))))


def _retention_kernel(q_ref, k_ref, v_ref, gate_ref, s0_ref, y_ref, sout_ref, state_s, *, mm_dtype):
    c = pl.program_id(1)

    @pl.when(c == 0)
    def _():
        state_s[...] = s0_ref[0]

    cl = q_ref.shape[1]
    dk = state_s.shape[1]
    ii = lax.broadcasted_iota(I32, (cl, cl), 0)
    jj = lax.broadcasted_iota(I32, (cl, cl), 1)
    dist = (ii - jj).astype(F32)
    ri = lax.broadcasted_iota(I32, (cl, 1), 0).astype(F32)
    for h in range(H_RET):
        log_g = _retention_log_decay(h)
        sl = slice(h * dk, (h + 1) * dk)
        q = q_ref[0, :, sl]
        k = k_ref[0, :, sl]
        decay = jnp.where(dist >= 0, jnp.exp(jnp.maximum(dist, 0.0) * log_g), 0.0)
        row_decay = jnp.exp((ri + 1.0) * log_g)
        col_decay = jnp.exp((cl - 1.0 - ri) * log_g)
        qm = q.astype(mm_dtype)
        vm = v_ref[0, :, sl].astype(mm_dtype)
        state = state_s[h]
        sc = lax.dot_general(qm, k.astype(mm_dtype), (((1,), (1,)), ((), ())), preferred_element_type=F32) * decay
        o = jnp.dot(sc.astype(mm_dtype), vm, preferred_element_type=F32)
        o = o + jnp.dot(qm, state.astype(mm_dtype), preferred_element_type=F32) * row_decay
        kw = (k * col_decay).astype(mm_dtype)
        upd = lax.dot_general(kw, vm, (((0,), (0,)), ((), ())), preferred_element_type=F32)
        state_s[h] = math.exp(cl * log_g) * state + upd
        g = gate_ref[0, :, sl]
        y_ref[0, :, sl] = (g * jax.nn.sigmoid(g) * _rms(o)).astype(y_ref.dtype)

    @pl.when(c == pl.num_programs(1) - 1)
    def _():
        sout_ref[0] = state_s[...]


def _retention(q, k, v, gate, state0, chunk, y_dtype, mm_dtype):
    b, s, dm = q.shape
    nc = s // chunk
    blk = pl.BlockSpec((1, chunk, dm), lambda bi, c: (bi, c, 0))
    st = pl.BlockSpec((1,) + state0.shape[1:], lambda bi, c: (bi, 0, 0, 0))
    return pl.pallas_call(
        functools.partial(_retention_kernel, mm_dtype=mm_dtype),
        grid=(b, nc),
        in_specs=[blk, blk, blk, blk, st],
        out_specs=[blk, st],
        out_shape=[jax.ShapeDtypeStruct((b, s, dm), y_dtype), jax.ShapeDtypeStruct(state0.shape, F32)],
        scratch_shapes=[pltpu.VMEM(state0.shape[1:], F32)],
        compiler_params=_cparams(("parallel", "arbitrary")),
        name="retention",
    )(q, k, v, gate, state0)


def _mem_kv_kernel(mem_ref, g_ref, w_ref, o_ref):
    xn = _rms(mem_ref[...], g_ref[...]).astype(BF16)
    o_ref[...] = jnp.dot(xn, w_ref[...], preferred_element_type=F32)


def _mem_kv(mem, g, w):
    m, dm = mem.shape
    n = w.shape[1]
    return pl.pallas_call(
        _mem_kv_kernel,
        grid=(n // dm,),
        in_specs=[pl.BlockSpec((m, dm), lambda i: (0, 0)), pl.BlockSpec((1, dm), lambda i: (0, 0)),
                  pl.BlockSpec((dm, dm), lambda i: (0, i))],
        out_specs=pl.BlockSpec((m, dm), lambda i: (0, i)),
        out_shape=jax.ShapeDtypeStruct((m, n), F32),
        compiler_params=_cparams(("parallel",)),
        name="mem_kv",
    )(mem, g, w)


def _mem_attn_kernel(q_ref, mk_ref, mv_ref, o_ref):
    q = q_ref[0].astype(BF16)
    mk = mk_ref[0].astype(BF16)
    mv = mv_ref[0].astype(BF16)
    d = q.shape[-1] // H_MEM
    outs = []
    for h in range(H_MEM):
        sl = slice(h * d, (h + 1) * d)
        s = lax.dot_general(q[:, sl], mk[:, sl], (((1,), (1,)), ((), ())), preferred_element_type=F32)
        m = jnp.max(s, axis=-1, keepdims=True)
        p = jnp.exp(s - m)
        p = p / jnp.sum(p, axis=-1, keepdims=True)
        outs.append(jnp.dot(p.astype(BF16), mv[:, sl], preferred_element_type=F32))
    o_ref[0] = jnp.concatenate(outs, axis=-1).astype(o_ref.dtype)


def _mem_attn(q, mk, mv, y_dtype):
    b, t, dm = q.shape
    tm = _row_tile(t, 512)
    qb = pl.BlockSpec((1, tm, dm), lambda bi, i: (bi, i, 0))
    mb = pl.BlockSpec((1,) + mk.shape[1:], lambda bi, i: (bi, 0, 0))
    return pl.pallas_call(
        _mem_attn_kernel,
        grid=(b, t // tm),
        in_specs=[qb, mb, mb],
        out_specs=qb,
        out_shape=jax.ShapeDtypeStruct((b, t, dm), y_dtype),
        compiler_params=_cparams(("parallel", "parallel")),
        name="mem_attn",
    )(q, mk, mv)


def _merge_kernel(x_ref, yd_ref, yr_ref, ym_ref, gt_ref, wb_ref, wo_ref, g2_ref, rw_ref, rb_ref,
                  h1_ref, hn_ref, lg_ref):
    dm = x_ref.shape[-1]
    mixed = jnp.zeros(x_ref.shape, F32)
    for i, y_ref in enumerate((yd_ref, yr_ref, ym_ref)):
        proj = jnp.dot(y_ref[...].astype(BF16), wb_ref[i], preferred_element_type=F32)
        mixed = mixed + jax.nn.sigmoid(gt_ref[:, i * dm:(i + 1) * dm]) * proj
    h1 = x_ref[...] + jnp.dot(mixed.astype(BF16), wo_ref[...], preferred_element_type=F32)
    h1_ref[...] = h1
    hn = _rms(h1, g2_ref[...])
    hn_ref[...] = hn
    lg_ref[...] = jnp.dot(hn.astype(BF16), rw_ref[...], preferred_element_type=F32) + rb_ref[...]


def _merge(x, yd, yr, ym, gt, wb, wo, g2, rw, rb):
    t, dm = x.shape
    tm = _row_tile(t, 512)
    row = lambda i: (i, 0)
    blk = pl.BlockSpec((tm, dm), row)
    c2 = lambda i: (0, 0)
    return pl.pallas_call(
        _merge_kernel,
        grid=(t // tm,),
        in_specs=[blk, blk, blk, blk, pl.BlockSpec((tm, N_BRANCH * dm), row),
                  pl.BlockSpec(wb.shape, lambda i: (0, 0, 0)), pl.BlockSpec(wo.shape, c2),
                  pl.BlockSpec((1, dm), c2), pl.BlockSpec(rw.shape, c2), pl.BlockSpec(rb.shape, c2)],
        out_specs=[blk, blk, pl.BlockSpec((tm, V7X_LANES), row)],
        out_shape=[jax.ShapeDtypeStruct((t, dm), F32), jax.ShapeDtypeStruct((t, dm), F32),
                   jax.ShapeDtypeStruct((t, V7X_LANES), F32)],
        compiler_params=_cparams(("parallel",)),
        name="merge",
    )(x, yd, yr, ym, gt, wb, wo, g2, rw, rb)


def _route_kernel(lg_ref, idx_ref, gate_ref, rank_ref, cnt_ref, carry_s):
    i = pl.program_id(0)

    @pl.when(i == 0)
    def _():
        carry_s[...] = jnp.zeros(carry_s.shape, F32)

    l = lg_ref[...]
    tm = l.shape[0]
    lane = lax.broadcasted_iota(I32, l.shape, 1)
    vals, idxs, hots = [], [], []
    for _ in range(TOP_K):
        m = jnp.max(l, axis=-1, keepdims=True)
        ik = jnp.min(jnp.where(l == m, lane, V7X_LANES), axis=-1, keepdims=True)
        hot = lane == ik
        vals.append(m)
        idxs.append(ik)
        hots.append(hot)
        l = jnp.where(hot, NEG_INF, l)
    es = [jnp.exp(v - vals[0]) for v in vals]
    den = es[0] + es[1] + es[2] + es[3]
    picked = jnp.zeros(l.shape, F32)
    for hot in hots:
        picked = picked + jnp.where(hot, 1.0, 0.0)
    r = lax.broadcasted_iota(I32, (tm, tm), 0)
    c = lax.broadcasted_iota(I32, (tm, tm), 1)
    before = jnp.where(c < r, 1.0, 0.0).astype(BF16)
    cum = jnp.dot(before, picked.astype(BF16), preferred_element_type=F32) + carry_s[0:1, :]
    idx_o = jnp.zeros(l.shape, I32)
    gate_o = jnp.zeros(l.shape, F32)
    rank_o = jnp.zeros(l.shape, I32)
    for k in range(TOP_K):
        rk = jnp.sum(jnp.where(hots[k], cum, 0.0), axis=-1, keepdims=True).astype(I32)
        idx_o = jnp.where(lane == k, idxs[k], idx_o)
        gate_o = jnp.where(lane == k, es[k] / den, gate_o)
        rank_o = jnp.where(lane == k, rk, rank_o)
    idx_ref[...] = idx_o
    gate_ref[...] = gate_o
    rank_ref[...] = rank_o
    carry_s[...] = carry_s[...] + jnp.sum(picked, axis=0, keepdims=True)
    cnt_ref[...] = carry_s[...]


def _route(logits):
    t = logits.shape[0]
    tm = _row_tile(t, 256)
    row = lambda i: (i, 0)
    blk = pl.BlockSpec((tm, V7X_LANES), row)
    return pl.pallas_call(
        _route_kernel,
        grid=(t // tm,),
        in_specs=[blk],
        out_specs=[blk, blk, blk, pl.BlockSpec((8, V7X_LANES), lambda i: (0, 0))],
        out_shape=[jax.ShapeDtypeStruct((t, V7X_LANES), I32), jax.ShapeDtypeStruct((t, V7X_LANES), F32),
                   jax.ShapeDtypeStruct((t, V7X_LANES), I32), jax.ShapeDtypeStruct((8, V7X_LANES), F32)],
        scratch_shapes=[pltpu.VMEM((8, V7X_LANES), F32)],
        compiler_params=_cparams(("arbitrary",)),
        name="route",
    )(logits)


def _row_copy(src, src_row, dst, dst_row, sem):
    return pltpu.make_async_copy(src.at[pl.ds(src_row, 1)], dst.at[pl.ds(dst_row, 1)], sem)


def _dispatch_kernel(dest_ref, hn_ref, xs_in_ref, xs_ref, sem):
    del xs_in_ref
    tm = hn_ref.shape[0]

    def issue(r, c):
        for k in range(TOP_K):
            _row_copy(hn_ref, r, xs_ref, dest_ref[r * TOP_K + k], sem).start()
        return c

    lax.fori_loop(0, tm, issue, 0)

    def drain(r, c):
        for k in range(TOP_K):
            _row_copy(hn_ref, r, xs_ref, dest_ref[r * TOP_K + k], sem).wait()
        return c

    lax.fori_loop(0, tm, drain, 0)


def _dispatch(dest_flat, hn, xs):
    t, dm = hn.shape
    tm = _row_tile(t, 128)
    return pl.pallas_call(
        _dispatch_kernel,
        grid=(t // tm,),
        in_specs=[pl.BlockSpec((tm * TOP_K,), lambda i: (i,), memory_space=pltpu.SMEM),
                  pl.BlockSpec((tm, dm), lambda i: (i, 0)),
                  pl.BlockSpec(memory_space=pl.ANY)],
        out_specs=pl.BlockSpec(memory_space=pl.ANY),
        out_shape=jax.ShapeDtypeStruct(xs.shape, xs.dtype),
        scratch_shapes=[pltpu.SemaphoreType.DMA(())],
        input_output_aliases={2: 0},
        compiler_params=_cparams(("arbitrary",)),
        name="moe_dispatch",
    )(dest_flat, hn, xs)


def _expert_kernel(be_ref, nu_ref, xs_ref, w1g_ref, w1l_ref, b1g_ref, b1l_ref, w2_ref, b2_ref, o_ref):
    i = pl.program_id(0)

    @pl.when(i < nu_ref[0])
    def _():
        x = xs_ref[...].astype(BF16)
        ug = jnp.dot(x, w1g_ref[0], preferred_element_type=F32) + b1g_ref[0]
        ul = jnp.dot(x, w1l_ref[0], preferred_element_type=F32) + b1l_ref[0]
        x_glu = jnp.minimum(ug, SWIGLU_LIMIT)
        x_lin = jnp.clip(ul, -SWIGLU_LIMIT, SWIGLU_LIMIT)
        act = x_glu * jax.nn.sigmoid(SWIGLU_ALPHA * x_glu) * (x_lin + 1.0)
        o_ref[...] = jnp.dot(act.astype(BF16), w2_ref[0], preferred_element_type=F32) + b2_ref[0]

    @pl.when(i >= nu_ref[0])
    def _():
        o_ref[...] = jnp.zeros(o_ref.shape, F32)


def _experts(block_exp, n_used, xs, w1g, w1l, b1g, b1l, w2, b2):
    cap, dm = xs.shape
    dff = w1g.shape[-1]
    n_blocks = cap // MOE_BLOCK
    row = lambda i, be, nu: (i, 0)
    wsel = lambda i, be, nu: (be[i], 0, 0)
    grid_spec = pltpu.PrefetchScalarGridSpec(
        num_scalar_prefetch=2,
        grid=(n_blocks,),
        in_specs=[pl.BlockSpec((MOE_BLOCK, dm), row),
                  pl.BlockSpec((1, dm, dff), wsel), pl.BlockSpec((1, dm, dff), wsel),
                  pl.BlockSpec((1, 1, dff), wsel), pl.BlockSpec((1, 1, dff), wsel),
                  pl.BlockSpec((1, dff, dm), wsel), pl.BlockSpec((1, 1, dm), wsel)],
        out_specs=pl.BlockSpec((MOE_BLOCK, dm), row),
    )
    return pl.pallas_call(
        _expert_kernel,
        grid_spec=grid_spec,
        out_shape=jax.ShapeDtypeStruct((cap, dm), F32),
        compiler_params=_cparams(("arbitrary",)),
        name="moe_experts",
    )(block_exp, n_used, xs, w1g, w1l, b1g, b1l, w2, b2)


def _combine_kernel(dest_ref, gate_ref, h1_ref, gf_ref, eo_ref, y_ref, buf, sem):
    tm = h1_ref.shape[0]

    def issue(r, c):
        for k in range(TOP_K):
            _row_copy(eo_ref, dest_ref[r * TOP_K + k], buf.at[k], r, sem).start()
        return c

    lax.fori_loop(0, tm, issue, 0)

    def drain(r, c):
        for k in range(TOP_K):
            _row_copy(eo_ref, dest_ref[r * TOP_K + k], buf.at[k], r, sem).wait()
        return c

    lax.fori_loop(0, tm, drain, 0)
    gate = gate_ref[...]
    y = h1_ref[...]
    for k in range(TOP_K):
        y = y + gate[:, k:k + 1] * buf[k]
    y_ref[...] = _rms(y, gf_ref[...])


def _combine(dest_flat, gate, h1, gf, eo):
    t, dm = h1.shape
    tm = _row_tile(t, 128)
    row = lambda i: (i, 0)
    return pl.pallas_call(
        _combine_kernel,
        grid=(t // tm,),
        in_specs=[pl.BlockSpec((tm * TOP_K,), lambda i: (i,), memory_space=pltpu.SMEM),
                  pl.BlockSpec((tm, V7X_LANES), row), pl.BlockSpec((tm, dm), row),
                  pl.BlockSpec((1, dm), lambda i: (0, 0)), pl.BlockSpec(memory_space=pl.ANY)],
        out_specs=pl.BlockSpec((tm, dm), row),
        out_shape=jax.ShapeDtypeStruct((t, dm), F32),
        scratch_shapes=[pltpu.VMEM((TOP_K, tm, dm), F32), pltpu.SemaphoreType.DMA(())],
        compiler_params=_cparams(("arbitrary",)),
        name="moe_combine",
    )(dest_flat, gate, h1, gf, eo)


def _moe_and_final_norm(parts, router_b_unused, w1, b1, w2, b2, normf_g):
    del router_b_unused
    sizes = [p[0].shape[0] for p in parts]
    t = sum(sizes)
    dm = parts[0][0].shape[1]
    logits = jnp.concatenate([p[2] for p in parts], axis=0)
    idx, gate, rank, counts = _route(logits)
    cnt = counts[0, :N_EXPERTS].astype(I32)
    padded = ((cnt + MOE_BLOCK - 1) // MOE_BLOCK) * MOE_BLOCK
    pad_ends = jnp.cumsum(padded)
    pad_starts = pad_ends - padded
    n_blocks = -(-(t * TOP_K + N_EXPERTS * (MOE_BLOCK - 1)) // MOE_BLOCK)
    cap = n_blocks * MOE_BLOCK
    block_start = jnp.arange(n_blocks, dtype=I32) * MOE_BLOCK
    block_exp = jnp.minimum(jnp.sum(pad_ends[None, :] <= block_start[:, None], axis=1), N_EXPERTS - 1).astype(I32)
    n_used = (pad_ends[-1:] // MOE_BLOCK).astype(I32)
    dest = (pad_starts[idx[:, :TOP_K]] + rank[:, :TOP_K]).astype(I32).reshape(-1)

    xs = jnp.zeros((cap, dm), F32)
    off = 0
    for (h1, hn, _), n in zip(parts, sizes):
        xs = _dispatch(dest[off * TOP_K:(off + n) * TOP_K], hn, xs)
        off += n
    w1g = w1[:, :, 0::2].astype(BF16)
    w1l = w1[:, :, 1::2].astype(BF16)
    b1g = b1[:, None, 0::2]
    b1l = b1[:, None, 1::2]
    eo = _experts(block_exp, n_used, xs, w1g, w1l, b1g, b1l, w2.astype(BF16), b2[:, None, :])
    outs = []
    off = 0
    for (h1, hn, _), n in zip(parts, sizes):
        outs.append(_combine(dest[off * TOP_K:(off + n) * TOP_K], gate[off:off + n], h1, normf_g, eo))
        off += n
    return outs


def _rope_tables(pos, half):
    inv = ROPE_BASE ** (-jnp.arange(half, dtype=F32) / half)
    ang = pos.astype(F32)[:, None] * inv[None, :]
    return jnp.cos(ang), jnp.sin(ang)


def kernel(x_prompt, x_sample, cache_k, cache_v, state_ret, cache_mem_k, cache_mem_v, page_table, mem_prompt,
           norm1_g, w_in, lambda_q1, lambda_k1, lambda_q2, lambda_k2, subln_g, norm_mem_g, w_mem_kv, w_branch,
           w_out, norm2_g, router_w, router_b, w1, b1, w2, b2, normf_g):
    b, s, dm = x_prompt.shape
    db, t, _ = x_sample.shape
    depth = w_in.shape[0]
    assert depth == 1, "final norm is fused into the last layer's MoE combine; one layer supported"
    assert b == 1
    page = cache_k.shape[2]
    past_len = page_table.shape[1] * page
    dk_ret = dm // H_RET
    n_mem = mem_prompt.shape[1]

    cos_p, sin_p = _rope_tables(jnp.arange(s), dk_ret // 2)
    cos_s, sin_s = _rope_tables(past_len + jnp.arange(t), dk_ret // 2)
    cos_s = jnp.tile(cos_s, (db, 1))
    sin_s = jnp.tile(sin_s, (db, 1))

    l = 0
    lam_init = 0.8 - 0.6 * math.exp(-0.3 * l)
    lam4 = jnp.stack([lambda_q1[l], lambda_k1[l], lambda_q2[l], lambda_k2[l]]).astype(F32)
    g1 = norm1_g[l][None, :]
    sub_g = subln_g[l][None, :]
    d3 = 3 * dm
    w_l = w_in[l]
    w_diff = w_l[:, :d3].astype(BF16)
    w_ret = w_l[:, d3:d3 + 4 * dm].astype(BF16)
    w_mg = w_l[:, d3 + 4 * dm:].astype(BF16)
    wb = w_branch[l].astype(BF16)
    wo = w_out[l].astype(BF16)
    g2 = norm2_g[l][None, :]
    rw = jnp.zeros((dm, V7X_LANES), F32).at[:, :N_EXPERTS].set(router_w[l]).astype(BF16)
    rb = jnp.full((1, V7X_LANES), NEG_INF, F32).at[0, :N_EXPERTS].set(router_b[l])

    xp = x_prompt.reshape(s, dm)
    q_d, k_f, v_f, k_b, v_t = _proj_diff(xp, g1, w_diff, BF16)
    y_d = _diff_attn_prompt(q_d, k_b, v_t, lam4, subln_g[l][:, None], lam_init)
    rq, rk, rv, rg = _proj_ret(xp, g1, w_ret, cos_p, sin_p)
    to3 = lambda a: a.reshape(1, s, dm)
    y_r, st_p = _retention(to3(rq), to3(rk), to3(rv), to3(rg), jnp.zeros((1, H_RET, dk_ret, dk_ret), F32),
                           chunk=_row_tile(s, RET_CHUNK), y_dtype=BF16, mm_dtype=BF16)
    mq, gt = _proj_mem_gate(xp, g1, w_mg, BF16)
    mkv = _mem_kv(mem_prompt.reshape(n_mem, dm), norm_mem_g[l][None, :], w_mem_kv[l].astype(BF16))
    mk, mv = mkv[:, :dm], mkv[:, dm:]
    y_m = _mem_attn(mq[None], mk[None], mv[None], BF16)
    part_p = _merge(xp, y_d, y_r.reshape(s, dm), y_m.reshape(s, dm), gt, wb, wo, g2, rw, rb)

    ns = db * t
    xs_ = x_sample.reshape(ns, dm)
    q_s, ks_f, vs_f, _, _ = _proj_diff(xs_, g1, w_diff, F32)
    k_s5 = ks_f.reshape(db, t, H_DIFF, D_V)
    v_s5 = vs_f.reshape(db, t, H_DIFF, D_V)
    y_ds = _diff_attn_sample(page_table, q_s.reshape(db, t, dm), k_s5, v_s5, cache_k, cache_v, l,
                             lam4, sub_g, lam_init)
    rq, rk, rv, rg = _proj_ret(xs_, g1, w_ret, cos_s, sin_s)
    tos = lambda a: a.reshape(db, t, dm)
    y_rs, st_s = _retention(tos(rq), tos(rk), tos(rv), tos(rg), state_ret[l].astype(F32),
                            chunk=t, y_dtype=F32, mm_dtype=F32)
    mq_s, gt_s = _proj_mem_gate(xs_, g1, w_mg, F32)
    y_ms = _mem_attn(mq_s.reshape(db, t, dm), cache_mem_k[l].reshape(db, n_mem, dm),
                     cache_mem_v[l].reshape(db, n_mem, dm), F32)
    part_s = _merge(xs_, y_ds.reshape(ns, dm), y_rs.reshape(ns, dm), y_ms.reshape(ns, dm), gt_s, wb, wo, g2, rw, rb)

    y_p, y_s = _moe_and_final_norm([part_p, part_s], None, w1[l], b1[l], w2[l], b2[l], normf_g[None, :])

    return (y_p.reshape(b, s, dm), y_s.reshape(db, t, dm),
            k_f.reshape(1, b, s, H_DIFF, D_V), v_f.reshape(1, b, s, H_DIFF, D_V),
            st_p.reshape(1, b, H_RET, dk_ret, dk_ret),
            mk.reshape(1, b, n_mem, H_MEM, dm // H_MEM), mv.reshape(1, b, n_mem, H_MEM, dm // H_MEM),
            k_s5[None], v_s5[None], st_s[None].astype(state_ret.dtype))
```

```python
import functools
import math

import jax
import jax.numpy as jnp
import numpy as np
from jax import lax
from jax.experimental import pallas as pl
from jax.experimental.pallas import tpu as pltpu

F32 = jnp.float32
BF16 = jnp.bfloat16
I32 = jnp.int32

H_DIFF = 8
D_QK = 64
D_V = 128
H_RET = 4
H_MEM = 4
N_BRANCH = 3
N_EXPERTS = 32
TOP_K = 4
SWIGLU_LIMIT = 7.0
SWIGLU_ALPHA = 1.702
ROPE_BASE = 10000.0
RMS_EPS = 1e-6

V7X_LANES = 128
V7X_VMEM_LIMIT_BYTES = 56 * 1024 * 1024

MOE_BLOCK = 256
DMA_ISSUE_UNROLL = 4
RET_CHUNK = 256
NEG_INF = float("-inf")
DIFF_Q_SCALE = (D_QK ** -0.5) * math.log2(math.e)


def _cparams(sem):
    return pltpu.CompilerParams(dimension_semantics=sem, vmem_limit_bytes=V7X_VMEM_LIMIT_BYTES)


def _rms(x, g=None):
    y = x * lax.rsqrt(jnp.mean(x * x, axis=-1, keepdims=True) + RMS_EPS)
    return y if g is None else y * g


def _row_tile(n, pref):
    t = min(n, pref)
    while n % t:
        t //= 2
    return t


def _proj_diff_kernel(x_ref, g_ref, w_ref, q_ref, kf_ref, vf_ref, kb_ref, vt_ref):
    xn = _rms(x_ref[...], g_ref[...]).astype(BF16)
    d = kf_ref.shape[-1]
    q = jnp.dot(xn, w_ref[:, 0:d], preferred_element_type=F32)
    q_ref[...] = (q * DIFF_Q_SCALE).astype(q_ref.dtype)
    k = jnp.dot(xn, w_ref[:, d:2 * d], preferred_element_type=F32)
    kf_ref[...] = k
    kb_ref[...] = k.astype(BF16)
    v = jnp.dot(xn, w_ref[:, 2 * d:3 * d], preferred_element_type=F32)
    vf_ref[...] = v
    vt_ref[...] = v.T.astype(BF16)


def _proj_diff(x, g, w, q_dtype):
    t, dm = x.shape
    tm = _row_tile(t, 512)
    row = lambda i: (i, 0)
    blk = pl.BlockSpec((tm, dm), row)
    return pl.pallas_call(
        _proj_diff_kernel,
        grid=(t // tm,),
        in_specs=[blk, pl.BlockSpec((1, dm), lambda i: (0, 0)), pl.BlockSpec(w.shape, lambda i: (0, 0))],
        out_specs=[blk] * 4 + [pl.BlockSpec((dm, tm), lambda i: (0, i))],
        out_shape=[jax.ShapeDtypeStruct((t, dm), q_dtype), jax.ShapeDtypeStruct((t, dm), F32),
                   jax.ShapeDtypeStruct((t, dm), F32), jax.ShapeDtypeStruct((t, dm), BF16),
                   jax.ShapeDtypeStruct((dm, t), BF16)],
        compiler_params=_cparams(("parallel",)),
        name="proj_diff",
    )(x, g, w)


def _proj_ret_kernel(x_ref, g_ref, w_ref, cos_ref, sin_ref, q_ref, k_ref, v_ref, gate_ref):
    xn = _rms(x_ref[...], g_ref[...]).astype(BF16)
    d = q_ref.shape[-1]
    dk = d // H_RET
    half = dk // 2
    cos = cos_ref[...]
    sin = sin_ref[...]

    def rope(u, scale):
        outs = []
        for h in range(H_RET):
            x1 = u[:, h * dk:h * dk + half]
            x2 = u[:, h * dk + half:(h + 1) * dk]
            outs.append((x1 * cos - x2 * sin) * scale)
            outs.append((x1 * sin + x2 * cos) * scale)
        return jnp.concatenate(outs, axis=-1)

    q_ref[...] = rope(jnp.dot(xn, w_ref[:, 0:d], preferred_element_type=F32), 1.0)
    k_ref[...] = rope(jnp.dot(xn, w_ref[:, d:2 * d], preferred_element_type=F32), dk ** -0.5)
    v_ref[...] = jnp.dot(xn, w_ref[:, 2 * d:3 * d], preferred_element_type=F32)
    gate_ref[...] = jnp.dot(xn, w_ref[:, 3 * d:4 * d], preferred_element_type=F32)


def _proj_ret(x, g, w, cos, sin):
    t, dm = x.shape
    tm = _row_tile(t, 512)
    row = lambda i: (i, 0)
    blk = pl.BlockSpec((tm, dm), row)
    tab = pl.BlockSpec((tm, cos.shape[1]), row)
    return pl.pallas_call(
        _proj_ret_kernel,
        grid=(t // tm,),
        in_specs=[blk, pl.BlockSpec((1, dm), lambda i: (0, 0)), pl.BlockSpec(w.shape, lambda i: (0, 0)), tab, tab],
        out_specs=[blk] * 4,
        out_shape=[jax.ShapeDtypeStruct((t, dm), F32)] * 4,
        compiler_params=_cparams(("parallel",)),
        name="proj_ret",
    )(x, g, w, cos, sin)


def _proj_mem_gate_kernel(x_ref, g_ref, w_ref, mq_ref, gt_ref):
    xn = _rms(x_ref[...], g_ref[...]).astype(BF16)
    d = mq_ref.shape[-1]
    mq = jnp.dot(xn, w_ref[:, 0:d], preferred_element_type=F32)
    mq_ref[...] = (mq * ((d // H_MEM) ** -0.5)).astype(mq_ref.dtype)
    gt_ref[...] = jnp.dot(xn, w_ref[:, d:], preferred_element_type=F32)


def _proj_mem_gate(x, g, w, mq_dtype):
    t, dm = x.shape
    tm = _row_tile(t, 512)
    row = lambda i: (i, 0)
    return pl.pallas_call(
        _proj_mem_gate_kernel,
        grid=(t // tm,),
        in_specs=[pl.BlockSpec((tm, dm), row), pl.BlockSpec((1, dm), lambda i: (0, 0)),
                  pl.BlockSpec(w.shape, lambda i: (0, 0))],
        out_specs=[pl.BlockSpec((tm, dm), row), pl.BlockSpec((tm, N_BRANCH * dm), row)],
        out_shape=[jax.ShapeDtypeStruct((t, dm), mq_dtype), jax.ShapeDtypeStruct((t, N_BRANCH * dm), F32)],
        compiler_params=_cparams(("parallel",)),
        name="proj_mem_gate",
    )(x, g, w)


def _lambda_value(lam_ref, lam_init):
    t = lam_ref[...]
    a = jnp.sum(t[0:1] * t[1:2], axis=-1, keepdims=True)
    b = jnp.sum(t[2:3] * t[3:4], axis=-1, keepdims=True)
    return jnp.exp(a) - jnp.exp(b) + lam_init


def _diff_attn_prompt_kernel(q_ref, k_ref, vt_ref, lam_ref, g_ref, o_ref, m_s, l_s, acc_s, sa_s, sb_s, mba_s, mbb_s,
                             *, tq, tk, rc, lam_init):
    qi = pl.program_id(1)
    q = q_ref[...]
    lane = lax.broadcasted_iota(I32, q.shape, 1)
    zero = jnp.zeros_like(q)
    qz = jnp.concatenate([jnp.where(lane < D_QK, q, zero), jnp.where(lane >= D_QK, q, zero)], axis=0)
    m_s[...] = jnp.full(m_s.shape, NEG_INF, F32)
    l_s[...] = jnp.zeros(l_s.shape, F32)
    acc_s[...] = jnp.zeros(acc_s.shape, F32)
    n_full = (qi * tq) // tk

    def scores(j, masked, s_buf, mb_buf):
        ks = k_ref[pl.ds(pl.multiple_of(j * tk, tk), tk), :]
        s = lax.dot_general(ks, qz, (((1,), (1,)), ((), ())), preferred_element_type=F32)
        if masked:
            kpos = lax.broadcasted_iota(I32, s.shape, 0) + j * tk
            col = lax.broadcasted_iota(I32, s.shape, 1)
            qpos = jnp.where(col >= tq, col - tq, col) + qi * tq
            s = jnp.where(kpos <= qpos, s, NEG_INF)
        s_buf[...] = s
        mb_buf[...] = jnp.max(s, axis=0, keepdims=True)

    def absorb(s_buf, mb_buf, vblock):
        m_old = m_s[...]
        m_new = jnp.maximum(m_old, mb_buf[...])
        alpha = jnp.exp2(m_old - m_new)
        part = jnp.zeros((8, 2 * tq), F32)
        vstart = pl.multiple_of(vblock * tk, tk)
        pv = None
        for r in range(tk // rc):
            p = jnp.exp2(s_buf[r * rc:(r + 1) * rc, :] - m_new)
            part = part + jnp.sum(p.reshape(rc // 8, 8, 2 * tq), axis=0)
            vt = vt_ref[:, pl.ds(vstart + r * rc, rc)]
            d = jnp.dot(vt, p.astype(BF16), preferred_element_type=F32)
            pv = d if pv is None else pv + d
        l_s[...] = alpha * l_s[...] + jnp.sum(part, axis=0, keepdims=True)
        acc_s[...] = alpha * acc_s[...] + pv
        m_s[...] = m_new

    scores(n_full, True, sa_s, mba_s)
    n_pairs = n_full // 2

    def body(i, c):
        scores(2 * i, False, sb_s, mbb_s)
        absorb(sa_s, mba_s, jnp.where(i == 0, n_full, 2 * i - 1))
        scores(2 * i + 1, False, sa_s, mba_s)
        absorb(sb_s, mbb_s, 2 * i)
        return c

    lax.fori_loop(0, n_pairs, body, 0)
    pending = jnp.where(n_pairs == 0, n_full, 2 * n_pairs - 1)

    @pl.when(n_full % 2 == 1)
    def _():
        scores(n_full - 1, False, sb_s, mbb_s)
        absorb(sa_s, mba_s, pending)
        absorb(sb_s, mbb_s, n_full - 1)

    @pl.when(n_full % 2 == 0)
    def _():
        absorb(sa_s, mba_s, pending)

    o = acc_s[...] / l_s[...]
    lam = _lambda_value(lam_ref, lam_init)
    od = o[:, :tq] - lam * o[:, tq:]
    y = od * lax.rsqrt(jnp.mean(od * od, axis=0, keepdims=True) + RMS_EPS) * g_ref[...] * (1.0 - lam_init)
    o_ref[...] = y.T.astype(o_ref.dtype)


def _diff_attn_prompt(q, k, vt, lam4, subln_g_col, lam_init):
    s, dm = q.shape
    tq = _row_tile(s, 256)
    tk = _row_tile(s, 1024)
    rc = _row_tile(tk, 256)
    kern = functools.partial(_diff_attn_prompt_kernel, tq=tq, tk=tk, rc=rc, lam_init=lam_init)
    row_stat = pltpu.VMEM((1, 2 * tq), F32)
    score_buf = pltpu.VMEM((tk, 2 * tq), F32)
    return pl.pallas_call(
        kern,
        grid=(H_DIFF, s // tq),
        in_specs=[pl.BlockSpec((tq, D_V), lambda h, i: (i, h)),
                  pl.BlockSpec((s, D_V), lambda h, i: (0, h)),
                  pl.BlockSpec((D_V, s), lambda h, i: (h, 0)),
                  pl.BlockSpec(lam4.shape, lambda h, i: (0, 0)),
                  pl.BlockSpec((D_V, 1), lambda h, i: (0, 0))],
        out_specs=pl.BlockSpec((tq, D_V), lambda h, i: (i, h)),
        out_shape=jax.ShapeDtypeStruct((s, dm), BF16),
        scratch_shapes=[row_stat, row_stat, pltpu.VMEM((D_V, 2 * tq), F32),
                        score_buf, score_buf, row_stat, row_stat],
        compiler_params=_cparams(("parallel", "parallel")),
        name="diff_attn_prompt",
    )(q, k, vt, lam4, subln_g_col)


def _diff_attn_sample_kernel(pt_ref, q_ref, kn_ref, vn_ref, lam_ref, g_ref, *rest, pps, lam_init):
    k_refs = rest[:pps]
    v_refs = rest[pps:2 * pps]
    o_ref, qmat_s, bias_s, m_s, l_s, acc_s, s_s = rest[2 * pps:]
    j = pl.program_id(1)
    nq = q_ref.shape[1]
    pc = bias_s.shape[1]

    @pl.when(j == 0)
    def _():
        qb = q_ref[0]
        lane = lax.broadcasted_iota(I32, (nq, D_V), 1)
        parts = []
        for h in range(H_DIFF):
            blk = qb[:, h * D_V:(h + 1) * D_V]
            parts.append(jnp.where(lane < D_QK, blk, 0.0))
            parts.append(jnp.where(lane >= D_QK, blk, 0.0))
        qmat_s[...] = jnp.concatenate(parts, axis=0).astype(BF16)
        r = lax.broadcasted_iota(I32, bias_s.shape, 0)
        c = lax.broadcasted_iota(I32, bias_s.shape, 1)
        bias_s[...] = jnp.where((c % H_DIFF) == (r // (2 * nq)), 0.0, NEG_INF)
        m_s[...] = jnp.full(m_s.shape, NEG_INF, F32)
        l_s[...] = jnp.zeros(l_s.shape, F32)
        acc_s[...] = jnp.zeros(acc_s.shape, F32)

    def flat(page):
        return page.reshape(page.shape[0] * page.shape[1], page.shape[2]).astype(BF16)

    qmat = qmat_s[...]
    mb = None
    for r in range(pps):
        s = lax.dot_general(qmat, flat(k_refs[r][0, 0]), (((1,), (1,)), ((), ())),
                            preferred_element_type=F32) + bias_s[...]
        s_s[:, r * pc:(r + 1) * pc] = s
        mr = jnp.max(s, axis=-1, keepdims=True)
        mb = mr if mb is None else jnp.maximum(mb, mr)
    m_old = m_s[...]
    m_new = jnp.maximum(m_old, mb)
    alpha = jnp.exp2(m_old - m_new)
    lsum = None
    pv = None
    for r in range(pps):
        p = jnp.exp2(s_s[:, r * pc:(r + 1) * pc] - m_new)
        ls = jnp.sum(p, axis=-1, keepdims=True)
        lsum = ls if lsum is None else lsum + ls
        d = jnp.dot(p.astype(BF16), flat(v_refs[r][0, 0]), preferred_element_type=F32)
        pv = d if pv is None else pv + d
    l_s[...] = alpha * l_s[...] + lsum
    acc_s[...] = alpha * acc_s[...] + pv
    m_s[...] = m_new

    @pl.when(j == pl.num_programs(1) - 1)
    def _():
        s = lax.dot_general(qmat, flat(kn_ref[0]), (((1,), (1,)), ((), ())), preferred_element_type=F32)
        r = lax.broadcasted_iota(I32, s.shape, 0)
        c = lax.broadcasted_iota(I32, s.shape, 1)
        keep = ((c % H_DIFF) == (r // (2 * nq))) & ((c // H_DIFF) <= (r % nq))
        s = jnp.where(keep, s, NEG_INF)
        m_o = m_s[...]
        m_n = jnp.maximum(m_o, jnp.max(s, axis=-1, keepdims=True))
        al = jnp.exp2(m_o - m_n)
        p = jnp.exp2(s - m_n)
        l = al * l_s[...] + jnp.sum(p, axis=-1, keepdims=True)
        acc = al * acc_s[...] + jnp.dot(p.astype(BF16), flat(vn_ref[0]), preferred_element_type=F32)
        o = acc / l
        lam = _lambda_value(lam_ref, lam_init)
        g = g_ref[...]
        outs = []
        for h in range(H_DIFF):
            base = h * 2 * nq
            od = o[base:base + nq] - lam * o[base + nq:base + 2 * nq]
            outs.append(_rms(od, g) * (1.0 - lam_init))
        o_ref[0] = jnp.concatenate(outs, axis=-1)


def _diff_attn_sample(page_table, q, k_new, v_new, cache_k, cache_v, layer, lam4, subln_g, lam_init, pps=4):
    b, nq, dm = q.shape
    n_pages = page_table.shape[1]
    while n_pages % pps:
        pps //= 2
    page_shape = (1, 1) + cache_k.shape[2:]
    page_specs = [pl.BlockSpec(page_shape, functools.partial(
        lambda bi, j, pt, r: (layer, pt[bi, j * pps + r], 0, 0, 0), r=r)) for r in range(pps)]
    rows = H_DIFF * 2 * nq
    cols = cache_k.shape[2] * cache_k.shape[3]
    kern = functools.partial(_diff_attn_sample_kernel, pps=pps, lam_init=lam_init)
    per_seq3 = lambda bi, j, pt: (bi, 0, 0)
    per_seq4 = lambda bi, j, pt: (bi, 0, 0, 0)
    const2 = lambda bi, j, pt: (0, 0)
    grid_spec = pltpu.PrefetchScalarGridSpec(
        num_scalar_prefetch=1,
        grid=(b, n_pages // pps),
        in_specs=[pl.BlockSpec((1, nq, dm), per_seq3),
                  pl.BlockSpec((1,) + k_new.shape[1:], per_seq4),
                  pl.BlockSpec((1,) + v_new.shape[1:], per_seq4),
                  pl.BlockSpec(lam4.shape, const2),
                  pl.BlockSpec((1, D_V), const2)] + page_specs + page_specs,
        out_specs=pl.BlockSpec((1, nq, dm), per_seq3),
        scratch_shapes=[pltpu.VMEM((rows, D_V), BF16), pltpu.VMEM((rows, cols), F32),
                        pltpu.VMEM((rows, 1), F32), pltpu.VMEM((rows, 1), F32), pltpu.VMEM((rows, D_V), F32),
                        pltpu.VMEM((rows, pps * cols), F32)],
    )
    return pl.pallas_call(
        kern,
        grid_spec=grid_spec,
        out_shape=jax.ShapeDtypeStruct((b, nq, dm), F32),
        compiler_params=_cparams(("parallel", "arbitrary")),
        name="diff_attn_sample",
    )(page_table, q, k_new, v_new, lam4, subln_g, *([cache_k] * pps), *([cache_v] * pps))


def _retention_log_decay(h):
    return float(np.log(np.float32(1.0) - np.exp2(np.float32(-5.0 - h))))


def _retention_kernel(q_ref, k_ref, v_ref, gate_ref, s0_ref, y_ref, sout_ref, state_s, *, mm_dtype):
    c = pl.program_id(1)

    @pl.when(c == 0)
    def _():
        state_s[...] = s0_ref[0]

    cl = q_ref.shape[1]
    dk = state_s.shape[1]
    ii = lax.broadcasted_iota(I32, (cl, cl), 0)
    jj = lax.broadcasted_iota(I32, (cl, cl), 1)
    dist = (ii - jj).astype(F32)
    ri = lax.broadcasted_iota(I32, (cl, 1), 0).astype(F32)
    for h in range(H_RET):
        log_g = _retention_log_decay(h)
        sl = slice(h * dk, (h + 1) * dk)
        q = q_ref[0, :, sl]
        k = k_ref[0, :, sl]
        decay = jnp.where(dist >= 0, jnp.exp(jnp.maximum(dist, 0.0) * log_g), 0.0)
        row_decay = jnp.exp((ri + 1.0) * log_g)
        col_decay = jnp.exp((cl - 1.0 - ri) * log_g)
        qm = q.astype(mm_dtype)
        vm = v_ref[0, :, sl].astype(mm_dtype)
        state = state_s[h]
        sc = lax.dot_general(qm, k.astype(mm_dtype), (((1,), (1,)), ((), ())), preferred_element_type=F32) * decay
        o = jnp.dot(sc.astype(mm_dtype), vm, preferred_element_type=F32)
        o = o + jnp.dot(qm, state.astype(mm_dtype), preferred_element_type=F32) * row_decay
        kw = (k * col_decay).astype(mm_dtype)
        upd = lax.dot_general(kw, vm, (((0,), (0,)), ((), ())), preferred_element_type=F32)
        state_s[h] = math.exp(cl * log_g) * state + upd
        g = gate_ref[0, :, sl]
        y_ref[0, :, sl] = (g * jax.nn.sigmoid(g) * _rms(o)).astype(y_ref.dtype)

    @pl.when(c == pl.num_programs(1) - 1)
    def _():
        sout_ref[0] = state_s[...]


def _retention(q, k, v, gate, state0, chunk, y_dtype, mm_dtype):
    b, s, dm = q.shape
    nc = s // chunk
    blk = pl.BlockSpec((1, chunk, dm), lambda bi, c: (bi, c, 0))
    st = pl.BlockSpec((1,) + state0.shape[1:], lambda bi, c: (bi, 0, 0, 0))
    return pl.pallas_call(
        functools.partial(_retention_kernel, mm_dtype=mm_dtype),
        grid=(b, nc),
        in_specs=[blk, blk, blk, blk, st],
        out_specs=[blk, st],
        out_shape=[jax.ShapeDtypeStruct((b, s, dm), y_dtype), jax.ShapeDtypeStruct(state0.shape, F32)],
        scratch_shapes=[pltpu.VMEM(state0.shape[1:], F32)],
        compiler_params=_cparams(("parallel", "arbitrary")),
        name="retention",
    )(q, k, v, gate, state0)


def _mem_kv_kernel(mem_ref, g_ref, w_ref, o_ref):
    xn = _rms(mem_ref[...], g_ref[...]).astype(BF16)
    o_ref[...] = jnp.dot(xn, w_ref[...], preferred_element_type=F32)


def _mem_kv(mem, g, w):
    m, dm = mem.shape
    n = w.shape[1]
    return pl.pallas_call(
        _mem_kv_kernel,
        grid=(n // dm,),
        in_specs=[pl.BlockSpec((m, dm), lambda i: (0, 0)), pl.BlockSpec((1, dm), lambda i: (0, 0)),
                  pl.BlockSpec((dm, dm), lambda i: (0, i))],
        out_specs=pl.BlockSpec((m, dm), lambda i: (0, i)),
        out_shape=jax.ShapeDtypeStruct((m, n), F32),
        compiler_params=_cparams(("parallel",)),
        name="mem_kv",
    )(mem, g, w)


def _mem_attn_kernel(q_ref, mk_ref, mv_ref, o_ref):
    q = q_ref[0].astype(BF16)
    mk = mk_ref[0].astype(BF16)
    mv = mv_ref[0].astype(BF16)
    d = q.shape[-1] // H_MEM
    outs = []
    for h in range(H_MEM):
        sl = slice(h * d, (h + 1) * d)
        s = lax.dot_general(q[:, sl], mk[:, sl], (((1,), (1,)), ((), ())), preferred_element_type=F32)
        m = jnp.max(s, axis=-1, keepdims=True)
        p = jnp.exp(s - m)
        p = p / jnp.sum(p, axis=-1, keepdims=True)
        outs.append(jnp.dot(p.astype(BF16), mv[:, sl], preferred_element_type=F32))
    o_ref[0] = jnp.concatenate(outs, axis=-1).astype(o_ref.dtype)


def _mem_attn(q, mk, mv, y_dtype):
    b, t, dm = q.shape
    tm = _row_tile(t, 512)
    qb = pl.BlockSpec((1, tm, dm), lambda bi, i: (bi, i, 0))
    mb = pl.BlockSpec((1,) + mk.shape[1:], lambda bi, i: (bi, 0, 0))
    return pl.pallas_call(
        _mem_attn_kernel,
        grid=(b, t // tm),
        in_specs=[qb, mb, mb],
        out_specs=qb,
        out_shape=jax.ShapeDtypeStruct((b, t, dm), y_dtype),
        compiler_params=_cparams(("parallel", "parallel")),
        name="mem_attn",
    )(q, mk, mv)


def _merge_kernel(x_ref, yd_ref, yr_ref, ym_ref, gt_ref, wb_ref, wo_ref, g2_ref, rw_ref, rb_ref,
                  h1_ref, hn_ref, lg_ref):
    dm = x_ref.shape[-1]
    mixed = jnp.zeros(x_ref.shape, F32)
    for i, y_ref in enumerate((yd_ref, yr_ref, ym_ref)):
        proj = jnp.dot(y_ref[...].astype(BF16), wb_ref[i], preferred_element_type=F32)
        mixed = mixed + jax.nn.sigmoid(gt_ref[:, i * dm:(i + 1) * dm]) * proj
    h1 = x_ref[...] + jnp.dot(mixed.astype(BF16), wo_ref[...], preferred_element_type=F32)
    h1_ref[...] = h1
    hn = _rms(h1, g2_ref[...])
    hn_ref[...] = hn
    lg_ref[...] = jnp.dot(hn.astype(BF16), rw_ref[...], preferred_element_type=F32) + rb_ref[...]


def _merge(x, yd, yr, ym, gt, wb, wo, g2, rw, rb):
    t, dm = x.shape
    tm = _row_tile(t, 512)
    row = lambda i: (i, 0)
    blk = pl.BlockSpec((tm, dm), row)
    c2 = lambda i: (0, 0)
    return pl.pallas_call(
        _merge_kernel,
        grid=(t // tm,),
        in_specs=[blk, blk, blk, blk, pl.BlockSpec((tm, N_BRANCH * dm), row),
                  pl.BlockSpec(wb.shape, lambda i: (0, 0, 0)), pl.BlockSpec(wo.shape, c2),
                  pl.BlockSpec((1, dm), c2), pl.BlockSpec(rw.shape, c2), pl.BlockSpec(rb.shape, c2)],
        out_specs=[blk, blk, pl.BlockSpec((tm, V7X_LANES), row)],
        out_shape=[jax.ShapeDtypeStruct((t, dm), F32), jax.ShapeDtypeStruct((t, dm), F32),
                   jax.ShapeDtypeStruct((t, V7X_LANES), F32)],
        compiler_params=_cparams(("parallel",)),
        name="merge",
    )(x, yd, yr, ym, gt, wb, wo, g2, rw, rb)


def _route_kernel(lg_ref, idx_ref, gate_ref, rank_ref, cnt_ref, carry_s):
    i = pl.program_id(0)

    @pl.when(i == 0)
    def _():
        carry_s[...] = jnp.zeros(carry_s.shape, F32)

    l = lg_ref[...]
    tm = l.shape[0]
    lane = lax.broadcasted_iota(I32, l.shape, 1)
    vals, idxs, hots = [], [], []
    for _ in range(TOP_K):
        m = jnp.max(l, axis=-1, keepdims=True)
        ik = jnp.min(jnp.where(l == m, lane, V7X_LANES), axis=-1, keepdims=True)
        hot = lane == ik
        vals.append(m)
        idxs.append(ik)
        hots.append(hot)
        l = jnp.where(hot, NEG_INF, l)
    es = [jnp.exp(v - vals[0]) for v in vals]
    den = es[0] + es[1] + es[2] + es[3]
    picked = jnp.zeros(l.shape, F32)
    for hot in hots:
        picked = picked + jnp.where(hot, 1.0, 0.0)
    r = lax.broadcasted_iota(I32, (tm, tm), 0)
    c = lax.broadcasted_iota(I32, (tm, tm), 1)
    before = jnp.where(c < r, 1.0, 0.0).astype(BF16)
    cum = jnp.dot(before, picked.astype(BF16), preferred_element_type=F32) + carry_s[0:1, :]
    idx_o = jnp.zeros(l.shape, I32)
    gate_o = jnp.zeros(l.shape, F32)
    rank_o = jnp.zeros(l.shape, I32)
    for k in range(TOP_K):
        rk = jnp.sum(jnp.where(hots[k], cum, 0.0), axis=-1, keepdims=True).astype(I32)
        idx_o = jnp.where(lane == k, idxs[k], idx_o)
        gate_o = jnp.where(lane == k, es[k] / den, gate_o)
        rank_o = jnp.where(lane == k, rk, rank_o)
    idx_ref[...] = idx_o
    gate_ref[...] = gate_o
    rank_ref[...] = rank_o
    carry_s[...] = carry_s[...] + jnp.sum(picked, axis=0, keepdims=True)
    cnt_ref[...] = carry_s[...]


def _route(logits):
    t = logits.shape[0]
    tm = _row_tile(t, 256)
    row = lambda i: (i, 0)
    blk = pl.BlockSpec((tm, V7X_LANES), row)
    return pl.pallas_call(
        _route_kernel,
        grid=(t // tm,),
        in_specs=[blk],
        out_specs=[blk, blk, blk, pl.BlockSpec((8, V7X_LANES), lambda i: (0, 0))],
        out_shape=[jax.ShapeDtypeStruct((t, V7X_LANES), I32), jax.ShapeDtypeStruct((t, V7X_LANES), F32),
                   jax.ShapeDtypeStruct((t, V7X_LANES), I32), jax.ShapeDtypeStruct((8, V7X_LANES), F32)],
        scratch_shapes=[pltpu.VMEM((8, V7X_LANES), F32)],
        compiler_params=_cparams(("arbitrary",)),
        name="route",
    )(logits)


def _row_copy(src, src_row, dst, dst_row, sem):
    return pltpu.make_async_copy(src.at[pl.ds(src_row, 1)], dst.at[pl.ds(dst_row, 1)], sem)


def _dispatch_kernel(dest_ref, hn_ref, xs_in_ref, xs_ref, sem):
    del xs_in_ref
    tm = hn_ref.shape[0]

    def issue(r, c):
        for k in range(TOP_K):
            _row_copy(hn_ref, r, xs_ref, dest_ref[r * TOP_K + k], sem).start()
        return c

    lax.fori_loop(0, tm, issue, 0, unroll=DMA_ISSUE_UNROLL)
    for k in range(TOP_K):
        pltpu.make_async_copy(hn_ref, xs_ref.at[pl.ds(0, tm)], sem).wait()


def _dispatch(dest_flat, hn, xs):
    t, dm = hn.shape
    tm = _row_tile(t, 128)
    return pl.pallas_call(
        _dispatch_kernel,
        grid=(t // tm,),
        in_specs=[pl.BlockSpec((tm * TOP_K,), lambda i: (i,), memory_space=pltpu.SMEM),
                  pl.BlockSpec((tm, dm), lambda i: (i, 0)),
                  pl.BlockSpec(memory_space=pl.ANY)],
        out_specs=pl.BlockSpec(memory_space=pl.ANY),
        out_shape=jax.ShapeDtypeStruct(xs.shape, xs.dtype),
        scratch_shapes=[pltpu.SemaphoreType.DMA(())],
        input_output_aliases={2: 0},
        compiler_params=_cparams(("arbitrary",)),
        name="moe_dispatch",
    )(dest_flat, hn, xs)


def _expert_kernel(be_ref, nu_ref, xs_ref, w1_ref, b1g_ref, b1l_ref, w2_ref, b2_ref, o_ref, w1g_s, w1l_s, w2_s):
    i = pl.program_id(0)
    e = be_ref[i]
    prev = be_ref[jnp.maximum(i - 1, 0)]

    @pl.when((i == 0) | (e != prev))
    def _():
        cw = 2 * V7X_LANES
        r = lax.broadcasted_iota(I32, (cw, cw), 0)
        c = lax.broadcasted_iota(I32, (cw, cw), 1)
        src_col = jnp.where(c < V7X_LANES, 2 * c, 2 * (c - V7X_LANES) + 1)
        sel = jnp.where(r == src_col, 1.0, 0.0).astype(BF16)
        for j in range(w1_ref.shape[2] // cw):
            t = jnp.dot(w1_ref[0, :, j * cw:(j + 1) * cw].astype(BF16), sel, preferred_element_type=F32)
            w1g_s[:, j * V7X_LANES:(j + 1) * V7X_LANES] = t[:, :V7X_LANES].astype(BF16)
            w1l_s[:, j * V7X_LANES:(j + 1) * V7X_LANES] = t[:, V7X_LANES:].astype(BF16)
        w2_s[...] = w2_ref[0].astype(BF16)

    @pl.when(i < nu_ref[0])
    def _():
        x = xs_ref[...].astype(BF16)
        ug = jnp.dot(x, w1g_s[...], preferred_element_type=F32) + b1g_ref[0]
        ul = jnp.dot(x, w1l_s[...], preferred_element_type=F32) + b1l_ref[0]
        x_glu = jnp.minimum(ug, SWIGLU_LIMIT)
        x_lin = jnp.clip(ul, -SWIGLU_LIMIT, SWIGLU_LIMIT)
        act = x_glu * jax.nn.sigmoid(SWIGLU_ALPHA * x_glu) * (x_lin + 1.0)
        o_ref[...] = jnp.dot(act.astype(BF16), w2_s[...], preferred_element_type=F32) + b2_ref[0]

    @pl.when(i >= nu_ref[0])
    def _():
        o_ref[...] = jnp.zeros(o_ref.shape, F32)


def _experts(block_exp, n_used, xs, w1, b1g, b1l, w2, b2):
    cap, dm = xs.shape
    dff = w2.shape[1]
    n_blocks = cap // MOE_BLOCK
    row_in = lambda i, be, nu: (jnp.minimum(i, nu[0] - 1), 0)
    row_out = lambda i, be, nu: (i, 0)
    wsel = lambda i, be, nu: (be[i], 0, 0)
    grid_spec = pltpu.PrefetchScalarGridSpec(
        num_scalar_prefetch=2,
        grid=(n_blocks,),
        in_specs=[pl.BlockSpec((MOE_BLOCK, dm), row_in),
                  pl.BlockSpec((1, dm, 2 * dff), wsel),
                  pl.BlockSpec((1, 1, dff), wsel), pl.BlockSpec((1, 1, dff), wsel),
                  pl.BlockSpec((1, dff, dm), wsel), pl.BlockSpec((1, 1, dm), wsel)],
        out_specs=pl.BlockSpec((MOE_BLOCK, dm), row_out),
        scratch_shapes=[pltpu.VMEM((dm, dff), BF16), pltpu.VMEM((dm, dff), BF16), pltpu.VMEM((dff, dm), BF16)],
    )
    return pl.pallas_call(
        _expert_kernel,
        grid_spec=grid_spec,
        out_shape=jax.ShapeDtypeStruct((cap, dm), F32),
        compiler_params=_cparams(("arbitrary",)),
        name="moe_experts",
    )(block_exp, n_used, xs, w1, b1g, b1l, w2, b2)


def _combine_kernel(dest_ref, gate_ref, h1_ref, gf_ref, eo_ref, y_ref, buf, sem):
    tm = h1_ref.shape[0]

    def issue(r, c):
        for k in range(TOP_K):
            _row_copy(eo_ref, dest_ref[r * TOP_K + k], buf.at[k], r, sem).start()
        return c

    lax.fori_loop(0, tm, issue, 0, unroll=DMA_ISSUE_UNROLL)
    for k in range(TOP_K):
        pltpu.make_async_copy(eo_ref.at[pl.ds(0, tm)], buf.at[k], sem).wait()
    gate = gate_ref[...]
    y = h1_ref[...]
    for k in range(TOP_K):
        y = y + gate[:, k:k + 1] * buf[k]
    y_ref[...] = _rms(y, gf_ref[...])


def _combine(dest_flat, gate, h1, gf, eo):
    t, dm = h1.shape
    tm = _row_tile(t, 128)
    row = lambda i: (i, 0)
    return pl.pallas_call(
        _combine_kernel,
        grid=(t // tm,),
        in_specs=[pl.BlockSpec((tm * TOP_K,), lambda i: (i,), memory_space=pltpu.SMEM),
                  pl.BlockSpec((tm, V7X_LANES), row), pl.BlockSpec((tm, dm), row),
                  pl.BlockSpec((1, dm), lambda i: (0, 0)), pl.BlockSpec(memory_space=pl.ANY)],
        out_specs=pl.BlockSpec((tm, dm), row),
        out_shape=jax.ShapeDtypeStruct((t, dm), F32),
        scratch_shapes=[pltpu.VMEM((TOP_K, tm, dm), F32), pltpu.SemaphoreType.DMA(())],
        compiler_params=_cparams(("arbitrary",)),
        name="moe_combine",
    )(dest_flat, gate, h1, gf, eo)


def _moe_and_final_norm(parts, router_b_unused, w1, b1, w2, b2, normf_g):
    del router_b_unused
    sizes = [p[0].shape[0] for p in parts]
    t = sum(sizes)
    dm = parts[0][0].shape[1]
    logits = jnp.concatenate([p[2] for p in parts], axis=0)
    idx, gate, rank, counts = _route(logits)
    cnt = counts[0, :N_EXPERTS].astype(I32)
    padded = ((cnt + MOE_BLOCK - 1) // MOE_BLOCK) * MOE_BLOCK
    pad_ends = jnp.cumsum(padded)
    pad_starts = pad_ends - padded
    n_blocks = -(-(t * TOP_K + N_EXPERTS * (MOE_BLOCK - 1)) // MOE_BLOCK)
    cap = n_blocks * MOE_BLOCK
    block_start = jnp.arange(n_blocks, dtype=I32) * MOE_BLOCK
    block_exp = jnp.minimum(jnp.sum(pad_ends[None, :] <= block_start[:, None], axis=1), N_EXPERTS - 1).astype(I32)
    n_used = (pad_ends[-1:] // MOE_BLOCK).astype(I32)
    dest = (pad_starts[idx[:, :TOP_K]] + rank[:, :TOP_K]).astype(I32).reshape(-1)

    xs = jnp.zeros((cap, dm), F32)
    off = 0
    for (h1, hn, _), n in zip(parts, sizes):
        xs = _dispatch(dest[off * TOP_K:(off + n) * TOP_K], hn, xs)
        off += n
    b1g = b1[:, None, 0::2]
    b1l = b1[:, None, 1::2]
    eo = _experts(block_exp, n_used, xs, w1, b1g, b1l, w2, b2[:, None, :])
    outs = []
    off = 0
    for (h1, hn, _), n in zip(parts, sizes):
        outs.append(_combine(dest[off * TOP_K:(off + n) * TOP_K], gate[off:off + n], h1, normf_g, eo))
        off += n
    return outs


def _rope_tables(pos, half):
    inv = ROPE_BASE ** (-jnp.arange(half, dtype=F32) / half)
    ang = pos.astype(F32)[:, None] * inv[None, :]
    return jnp.cos(ang), jnp.sin(ang)


def kernel(x_prompt, x_sample, cache_k, cache_v, state_ret, cache_mem_k, cache_mem_v, page_table, mem_prompt,
           norm1_g, w_in, lambda_q1, lambda_k1, lambda_q2, lambda_k2, subln_g, norm_mem_g, w_mem_kv, w_branch,
           w_out, norm2_g, router_w, router_b, w1, b1, w2, b2, normf_g):
    b, s, dm = x_prompt.shape
    db, t, _ = x_sample.shape
    depth = w_in.shape[0]
    assert depth == 1, "final norm is fused into the last layer's MoE combine; one layer supported"
    assert b == 1
    page = cache_k.shape[2]
    past_len = page_table.shape[1] * page
    dk_ret = dm // H_RET
    n_mem = mem_prompt.shape[1]

    cos_p, sin_p = _rope_tables(jnp.arange(s), dk_ret // 2)
    cos_s, sin_s = _rope_tables(past_len + jnp.arange(t), dk_ret // 2)
    cos_s = jnp.tile(cos_s, (db, 1))
    sin_s = jnp.tile(sin_s, (db, 1))

    l = 0
    lam_init = 0.8 - 0.6 * math.exp(-0.3 * l)
    lam4 = jnp.stack([lambda_q1[l], lambda_k1[l], lambda_q2[l], lambda_k2[l]]).astype(F32)
    g1 = norm1_g[l][None, :]
    sub_g = subln_g[l][None, :]
    d3 = 3 * dm
    w_l = w_in[l]
    w_diff = w_l[:, :d3].astype(BF16)
    w_ret = w_l[:, d3:d3 + 4 * dm].astype(BF16)
    w_mg = w_l[:, d3 + 4 * dm:].astype(BF16)
    wb = w_branch[l].astype(BF16)
    wo = w_out[l].astype(BF16)
    g2 = norm2_g[l][None, :]
    rw = jnp.zeros((dm, V7X_LANES), F32).at[:, :N_EXPERTS].set(router_w[l]).astype(BF16)
    rb = jnp.full((1, V7X_LANES), NEG_INF, F32).at[0, :N_EXPERTS].set(router_b[l])

    xp = x_prompt.reshape(s, dm)
    q_d, k_f, v_f, k_b, v_t = _proj_diff(xp, g1, w_diff, BF16)
    y_d = _diff_attn_prompt(q_d, k_b, v_t, lam4, subln_g[l][:, None], lam_init)
    rq, rk, rv, rg = _proj_ret(xp, g1, w_ret, cos_p, sin_p)
    to3 = lambda a: a.reshape(1, s, dm)
    y_r, st_p = _retention(to3(rq), to3(rk), to3(rv), to3(rg), jnp.zeros((1, H_RET, dk_ret, dk_ret), F32),
                           chunk=_row_tile(s, RET_CHUNK), y_dtype=BF16, mm_dtype=BF16)
    mq, gt = _proj_mem_gate(xp, g1, w_mg, BF16)
    mkv = _mem_kv(mem_prompt.reshape(n_mem, dm), norm_mem_g[l][None, :], w_mem_kv[l].astype(BF16))
    mk, mv = mkv[:, :dm], mkv[:, dm:]
    y_m = _mem_attn(mq[None], mk[None], mv[None], BF16)
    part_p = _merge(xp, y_d, y_r.reshape(s, dm), y_m.reshape(s, dm), gt, wb, wo, g2, rw, rb)

    ns = db * t
    xs_ = x_sample.reshape(ns, dm)
    q_s, ks_f, vs_f, _, _ = _proj_diff(xs_, g1, w_diff, F32)
    k_s5 = ks_f.reshape(db, t, H_DIFF, D_V)
    v_s5 = vs_f.reshape(db, t, H_DIFF, D_V)
    y_ds = _diff_attn_sample(page_table, q_s.reshape(db, t, dm), k_s5, v_s5, cache_k, cache_v, l,
                             lam4, sub_g, lam_init)
    rq, rk, rv, rg = _proj_ret(xs_, g1, w_ret, cos_s, sin_s)
    tos = lambda a: a.reshape(db, t, dm)
    y_rs, st_s = _retention(tos(rq), tos(rk), tos(rv), tos(rg), state_ret[l].astype(F32),
                            chunk=t, y_dtype=F32, mm_dtype=F32)
    mq_s, gt_s = _proj_mem_gate(xs_, g1, w_mg, F32)
    y_ms = _mem_attn(mq_s.reshape(db, t, dm), cache_mem_k[l].reshape(db, n_mem, dm),
                     cache_mem_v[l].reshape(db, n_mem, dm), F32)
    part_s = _merge(xs_, y_ds.reshape(ns, dm), y_rs.reshape(ns, dm), y_ms.reshape(ns, dm), gt_s, wb, wo, g2, rw, rb)

    y_p, y_s = _moe_and_final_norm([part_p, part_s], None, w1[l], b1[l], w2[l], b2[l], normf_g[None, :])

    return (y_p.reshape(b, s, dm), y_s.reshape(db, t, dm),
            k_f.reshape(1, b, s, H_DIFF, D_V), v_f.reshape(1, b, s, H_DIFF, D_V),
            st_p.reshape(1, b, H_RET, dk_ret, dk_ret),
            mk.reshape(1, b, n_mem, H_MEM, dm // H_MEM), mv.reshape(1, b, n_mem, H_MEM, dm // H_MEM),
            k_s5[None], v_s5[None], st_s[None].astype(state_ret.dtype))
```

```python
import functools
import math

import jax
import jax.numpy as jnp
import numpy as np
from jax import lax
from jax.experimental import pallas as pl
from jax.experimental.pallas import tpu as pltpu

F32 = jnp.float32
BF16 = jnp.bfloat16
I32 = jnp.int32

H_DIFF = 8
D_QK = 64
D_V = 128
H_RET = 4
H_MEM = 4
N_BRANCH = 3
N_EXPERTS = 32
TOP_K = 4
SWIGLU_LIMIT = 7.0
SWIGLU_ALPHA = 1.702
ROPE_BASE = 10000.0
RMS_EPS = 1e-6

V7X_LANES = 128
V7X_VMEM_LIMIT_BYTES = 56 * 1024 * 1024

MOE_BLOCK = 256
DMA_ISSUE_UNROLL = 16
SAMPLE_PAGES_PER_STEP = 16
SAMPLE_PAGES_PER_UPDATE = 4
RET_CHUNK = 256
NEG_INF = float("-inf")
DIFF_Q_SCALE = (D_QK ** -0.5) * math.log2(math.e)


def _cparams(sem):
    return pltpu.CompilerParams(dimension_semantics=sem, vmem_limit_bytes=V7X_VMEM_LIMIT_BYTES)


def _rms(x, g=None):
    y = x * lax.rsqrt(jnp.mean(x * x, axis=-1, keepdims=True) + RMS_EPS)
    return y if g is None else y * g


def _row_tile(n, pref):
    t = min(n, pref)
    while n % t:
        t //= 2
    return t


def _proj_diff_kernel(x_ref, g_ref, w_ref, q_ref, kf_ref, vf_ref, kb_ref, vt_ref):
    xn = _rms(x_ref[...], g_ref[...]).astype(BF16)
    d = kf_ref.shape[-1]
    q = jnp.dot(xn, w_ref[:, 0:d], preferred_element_type=F32)
    q_ref[...] = (q * DIFF_Q_SCALE).astype(q_ref.dtype)
    k = jnp.dot(xn, w_ref[:, d:2 * d], preferred_element_type=F32)
    kf_ref[...] = k
    kb_ref[...] = k.astype(BF16)
    v = jnp.dot(xn, w_ref[:, 2 * d:3 * d], preferred_element_type=F32)
    vf_ref[...] = v
    vt_ref[...] = v.T.astype(BF16)


def _proj_diff(x, g, w, q_dtype):
    t, dm = x.shape
    tm = _row_tile(t, 512)
    row = lambda i: (i, 0)
    blk = pl.BlockSpec((tm, dm), row)
    return pl.pallas_call(
        _proj_diff_kernel,
        grid=(t // tm,),
        in_specs=[blk, pl.BlockSpec((1, dm), lambda i: (0, 0)), pl.BlockSpec(w.shape, lambda i: (0, 0))],
        out_specs=[blk] * 4 + [pl.BlockSpec((dm, tm), lambda i: (0, i))],
        out_shape=[jax.ShapeDtypeStruct((t, dm), q_dtype), jax.ShapeDtypeStruct((t, dm), F32),
                   jax.ShapeDtypeStruct((t, dm), F32), jax.ShapeDtypeStruct((t, dm), BF16),
                   jax.ShapeDtypeStruct((dm, t), BF16)],
        compiler_params=_cparams(("parallel",)),
        name="proj_diff",
    )(x, g, w)


def _proj_ret_kernel(x_ref, g_ref, w_ref, cos_ref, sin_ref, q_ref, k_ref, v_ref, gate_ref):
    xn = _rms(x_ref[...], g_ref[...]).astype(BF16)
    d = q_ref.shape[-1]
    dk = d // H_RET
    half = dk // 2
    cos = cos_ref[...]
    sin = sin_ref[...]

    def rope(u, scale):
        outs = []
        for h in range(H_RET):
            x1 = u[:, h * dk:h * dk + half]
            x2 = u[:, h * dk + half:(h + 1) * dk]
            outs.append((x1 * cos - x2 * sin) * scale)
            outs.append((x1 * sin + x2 * cos) * scale)
        return jnp.concatenate(outs, axis=-1)

    q_ref[...] = rope(jnp.dot(xn, w_ref[:, 0:d], preferred_element_type=F32), 1.0)
    k_ref[...] = rope(jnp.dot(xn, w_ref[:, d:2 * d], preferred_element_type=F32), dk ** -0.5)
    v_ref[...] = jnp.dot(xn, w_ref[:, 2 * d:3 * d], preferred_element_type=F32)
    gate_ref[...] = jnp.dot(xn, w_ref[:, 3 * d:4 * d], preferred_element_type=F32)


def _proj_ret(x, g, w, cos, sin):
    t, dm = x.shape
    tm = _row_tile(t, 512)
    row = lambda i: (i, 0)
    blk = pl.BlockSpec((tm, dm), row)
    tab = pl.BlockSpec((tm, cos.shape[1]), row)
    return pl.pallas_call(
        _proj_ret_kernel,
        grid=(t // tm,),
        in_specs=[blk, pl.BlockSpec((1, dm), lambda i: (0, 0)), pl.BlockSpec(w.shape, lambda i: (0, 0)), tab, tab],
        out_specs=[blk] * 4,
        out_shape=[jax.ShapeDtypeStruct((t, dm), F32)] * 4,
        compiler_params=_cparams(("parallel",)),
        name="proj_ret",
    )(x, g, w, cos, sin)


def _proj_mem_gate_kernel(x_ref, g_ref, w_ref, mq_ref, gt_ref):
    xn = _rms(x_ref[...], g_ref[...]).astype(BF16)
    d = mq_ref.shape[-1]
    mq = jnp.dot(xn, w_ref[:, 0:d], preferred_element_type=F32)
    mq_ref[...] = (mq * ((d // H_MEM) ** -0.5)).astype(mq_ref.dtype)
    gt_ref[...] = jnp.dot(xn, w_ref[:, d:], preferred_element_type=F32)


def _proj_mem_gate(x, g, w, mq_dtype):
    t, dm = x.shape
    tm = _row_tile(t, 512)
    row = lambda i: (i, 0)
    return pl.pallas_call(
        _proj_mem_gate_kernel,
        grid=(t // tm,),
        in_specs=[pl.BlockSpec((tm, dm), row), pl.BlockSpec((1, dm), lambda i: (0, 0)),
                  pl.BlockSpec(w.shape, lambda i: (0, 0))],
        out_specs=[pl.BlockSpec((tm, dm), row), pl.BlockSpec((tm, N_BRANCH * dm), row)],
        out_shape=[jax.ShapeDtypeStruct((t, dm), mq_dtype), jax.ShapeDtypeStruct((t, N_BRANCH * dm), F32)],
        compiler_params=_cparams(("parallel",)),
        name="proj_mem_gate",
    )(x, g, w)


def _lambda_value(lam_ref, lam_init):
    t = lam_ref[...]
    a = jnp.sum(t[0:1] * t[1:2], axis=-1, keepdims=True)
    b = jnp.sum(t[2:3] * t[3:4], axis=-1, keepdims=True)
    return jnp.exp(a) - jnp.exp(b) + lam_init


def _diff_attn_prompt_kernel(q_ref, k_ref, vt_ref, lam_ref, g_ref, o_ref, m_s, l_s, acc_s, sa_s, sb_s, mba_s, mbb_s,
                             *, tq, tk, rc, lam_init):
    qi = pl.program_id(1)
    q = q_ref[...]
    lane = lax.broadcasted_iota(I32, q.shape, 1)
    zero = jnp.zeros_like(q)
    qz = jnp.concatenate([jnp.where(lane < D_QK, q, zero), jnp.where(lane >= D_QK, q, zero)], axis=0)
    m_s[...] = jnp.full(m_s.shape, NEG_INF, F32)
    l_s[...] = jnp.zeros(l_s.shape, F32)
    acc_s[...] = jnp.zeros(acc_s.shape, F32)
    n_full = (qi * tq) // tk

    def scores(j, masked, s_buf, mb_buf):
        ks = k_ref[pl.ds(pl.multiple_of(j * tk, tk), tk), :]
        s = lax.dot_general(ks, qz, (((1,), (1,)), ((), ())), preferred_element_type=F32)
        if masked:
            kpos = lax.broadcasted_iota(I32, s.shape, 0) + j * tk
            col = lax.broadcasted_iota(I32, s.shape, 1)
            qpos = jnp.where(col >= tq, col - tq, col) + qi * tq
            s = jnp.where(kpos <= qpos, s, NEG_INF)
        s_buf[...] = s
        mb_buf[...] = jnp.max(s, axis=0, keepdims=True)

    def absorb(s_buf, mb_buf, vblock):
        m_old = m_s[...]
        m_new = jnp.maximum(m_old, mb_buf[...])
        alpha = jnp.exp2(m_old - m_new)
        part = jnp.zeros((8, 2 * tq), F32)
        vstart = pl.multiple_of(vblock * tk, tk)
        pv = None
        for r in range(tk // rc):
            p = jnp.exp2(s_buf[r * rc:(r + 1) * rc, :] - m_new)
            part = part + jnp.sum(p.reshape(rc // 8, 8, 2 * tq), axis=0)
            vt = vt_ref[:, pl.ds(vstart + r * rc, rc)]
            d = jnp.dot(vt, p.astype(BF16), preferred_element_type=F32)
            pv = d if pv is None else pv + d
        l_s[...] = alpha * l_s[...] + jnp.sum(part, axis=0, keepdims=True)
        acc_s[...] = alpha * acc_s[...] + pv
        m_s[...] = m_new

    scores(n_full, True, sa_s, mba_s)
    n_pairs = n_full // 2

    def body(i, c):
        scores(2 * i, False, sb_s, mbb_s)
        absorb(sa_s, mba_s, jnp.where(i == 0, n_full, 2 * i - 1))
        scores(2 * i + 1, False, sa_s, mba_s)
        absorb(sb_s, mbb_s, 2 * i)
        return c

    lax.fori_loop(0, n_pairs, body, 0)
    pending = jnp.where(n_pairs == 0, n_full, 2 * n_pairs - 1)

    @pl.when(n_full % 2 == 1)
    def _():
        scores(n_full - 1, False, sb_s, mbb_s)
        absorb(sa_s, mba_s, pending)
        absorb(sb_s, mbb_s, n_full - 1)

    @pl.when(n_full % 2 == 0)
    def _():
        absorb(sa_s, mba_s, pending)

    o = acc_s[...] / l_s[...]
    lam = _lambda_value(lam_ref, lam_init)
    od = o[:, :tq] - lam * o[:, tq:]
    y = od * lax.rsqrt(jnp.mean(od * od, axis=0, keepdims=True) + RMS_EPS) * g_ref[...] * (1.0 - lam_init)
    o_ref[...] = y.T.astype(o_ref.dtype)


def _diff_attn_prompt(q, k, vt, lam4, subln_g_col, lam_init):
    s, dm = q.shape
    tq = _row_tile(s, 512)
    tk = _row_tile(s, 1024)
    rc = _row_tile(tk, 256)
    kern = functools.partial(_diff_attn_prompt_kernel, tq=tq, tk=tk, rc=rc, lam_init=lam_init)
    row_stat = pltpu.VMEM((1, 2 * tq), F32)
    score_buf = pltpu.VMEM((tk, 2 * tq), F32)
    return pl.pallas_call(
        kern,
        grid=(H_DIFF, s // tq),
        in_specs=[pl.BlockSpec((tq, D_V), lambda h, i: (i, h)),
                  pl.BlockSpec((s, D_V), lambda h, i: (0, h)),
                  pl.BlockSpec((D_V, s), lambda h, i: (h, 0)),
                  pl.BlockSpec(lam4.shape, lambda h, i: (0, 0)),
                  pl.BlockSpec((D_V, 1), lambda h, i: (0, 0))],
        out_specs=pl.BlockSpec((tq, D_V), lambda h, i: (i, h)),
        out_shape=jax.ShapeDtypeStruct((s, dm), BF16),
        scratch_shapes=[row_stat, row_stat, pltpu.VMEM((D_V, 2 * tq), F32),
                        score_buf, score_buf, row_stat, row_stat],
        compiler_params=_cparams(("parallel", "parallel")),
        name="diff_attn_prompt",
    )(q, k, vt, lam4, subln_g_col)


def _diff_attn_sample_kernel(pt_ref, q_ref, kn_ref, vn_ref, lam_ref, g_ref, *rest, pps, lam_init):
    k_refs = rest[:pps]
    v_refs = rest[pps:2 * pps]
    o_ref, qmat_s, bias_s, m_s, l_s, acc_s, s_s = rest[2 * pps:]
    j = pl.program_id(1)
    nq = q_ref.shape[1]
    pc = bias_s.shape[1]

    @pl.when(j == 0)
    def _():
        qb = q_ref[0]
        lane = lax.broadcasted_iota(I32, (nq, D_V), 1)
        parts = []
        for h in range(H_DIFF):
            blk = qb[:, h * D_V:(h + 1) * D_V]
            parts.append(jnp.where(lane < D_QK, blk, 0.0))
            parts.append(jnp.where(lane >= D_QK, blk, 0.0))
        qmat_s[...] = jnp.concatenate(parts, axis=0).astype(BF16)
        r = lax.broadcasted_iota(I32, bias_s.shape, 0)
        c = lax.broadcasted_iota(I32, bias_s.shape, 1)
        bias_s[...] = jnp.where((c % H_DIFF) == (r // (2 * nq)), 0.0, NEG_INF)
        m_s[...] = jnp.full(m_s.shape, NEG_INF, F32)
        l_s[...] = jnp.zeros(l_s.shape, F32)
        acc_s[...] = jnp.zeros(acc_s.shape, F32)

    def flat(page):
        return page.reshape(page.shape[0] * page.shape[1], page.shape[2]).astype(BF16)

    qmat = qmat_s[...]
    grp = s_s.shape[1] // pc
    for g0 in range(0, pps, grp):
        mb = None
        for r in range(grp):
            s = lax.dot_general(qmat, flat(k_refs[g0 + r][0, 0]), (((1,), (1,)), ((), ())),
                                preferred_element_type=F32) + bias_s[...]
            s_s[:, r * pc:(r + 1) * pc] = s
            mr = jnp.max(s, axis=-1, keepdims=True)
            mb = mr if mb is None else jnp.maximum(mb, mr)
        m_old = m_s[...]
        m_new = jnp.maximum(m_old, mb)
        alpha = jnp.exp2(m_old - m_new)
        lsum = None
        pv = None
        for r in range(grp):
            p = jnp.exp2(s_s[:, r * pc:(r + 1) * pc] - m_new)
            ls = jnp.sum(p, axis=-1, keepdims=True)
            lsum = ls if lsum is None else lsum + ls
            d = jnp.dot(p.astype(BF16), flat(v_refs[g0 + r][0, 0]), preferred_element_type=F32)
            pv = d if pv is None else pv + d
        l_s[...] = alpha * l_s[...] + lsum
        acc_s[...] = alpha * acc_s[...] + pv
        m_s[...] = m_new

    @pl.when(j == pl.num_programs(1) - 1)
    def _():
        s = lax.dot_general(qmat, flat(kn_ref[0]), (((1,), (1,)), ((), ())), preferred_element_type=F32)
        r = lax.broadcasted_iota(I32, s.shape, 0)
        c = lax.broadcasted_iota(I32, s.shape, 1)
        keep = ((c % H_DIFF) == (r // (2 * nq))) & ((c // H_DIFF) <= (r % nq))
        s = jnp.where(keep, s, NEG_INF)
        m_o = m_s[...]
        m_n = jnp.maximum(m_o, jnp.max(s, axis=-1, keepdims=True))
        al = jnp.exp2(m_o - m_n)
        p = jnp.exp2(s - m_n)
        l = al * l_s[...] + jnp.sum(p, axis=-1, keepdims=True)
        acc = al * acc_s[...] + jnp.dot(p.astype(BF16), flat(vn_ref[0]), preferred_element_type=F32)
        o = acc / l
        lam = _lambda_value(lam_ref, lam_init)
        g = g_ref[...]
        outs = []
        for h in range(H_DIFF):
            base = h * 2 * nq
            od = o[base:base + nq] - lam * o[base + nq:base + 2 * nq]
            outs.append(_rms(od, g) * (1.0 - lam_init))
        o_ref[0] = jnp.concatenate(outs, axis=-1)


def _diff_attn_sample(page_table, q, k_new, v_new, cache_k, cache_v, layer, lam4, subln_g, lam_init,
                      pps=SAMPLE_PAGES_PER_STEP, grp=SAMPLE_PAGES_PER_UPDATE):
    b, nq, dm = q.shape
    n_pages = page_table.shape[1]
    while n_pages % pps:
        pps //= 2
    grp = min(grp, pps)
    page_shape = (1, 1) + cache_k.shape[2:]
    page_specs = [pl.BlockSpec(page_shape, functools.partial(
        lambda bi, j, pt, r: (layer, pt[bi, j * pps + r], 0, 0, 0), r=r)) for r in range(pps)]
    rows = H_DIFF * 2 * nq
    cols = cache_k.shape[2] * cache_k.shape[3]
    kern = functools.partial(_diff_attn_sample_kernel, pps=pps, lam_init=lam_init)
    per_seq3 = lambda bi, j, pt: (bi, 0, 0)
    per_seq4 = lambda bi, j, pt: (bi, 0, 0, 0)
    const2 = lambda bi, j, pt: (0, 0)
    grid_spec = pltpu.PrefetchScalarGridSpec(
        num_scalar_prefetch=1,
        grid=(b, n_pages // pps),
        in_specs=[pl.BlockSpec((1, nq, dm), per_seq3),
                  pl.BlockSpec((1,) + k_new.shape[1:], per_seq4),
                  pl.BlockSpec((1,) + v_new.shape[1:], per_seq4),
                  pl.BlockSpec(lam4.shape, const2),
                  pl.BlockSpec((1, D_V), const2)] + page_specs + page_specs,
        out_specs=pl.BlockSpec((1, nq, dm), per_seq3),
        scratch_shapes=[pltpu.VMEM((rows, D_V), BF16), pltpu.VMEM((rows, cols), F32),
                        pltpu.VMEM((rows, 1), F32), pltpu.VMEM((rows, 1), F32), pltpu.VMEM((rows, D_V), F32),
                        pltpu.VMEM((rows, grp * cols), F32)],
    )
    return pl.pallas_call(
        kern,
        grid_spec=grid_spec,
        out_shape=jax.ShapeDtypeStruct((b, nq, dm), F32),
        compiler_params=_cparams(("parallel", "arbitrary")),
        name="diff_attn_sample",
    )(page_table, q, k_new, v_new, lam4, subln_g, *([cache_k] * pps), *([cache_v] * pps))


def _retention_log_decay(h):
    return float(np.log(np.float32(1.0) - np.exp2(np.float32(-5.0 - h))))


def _retention_kernel(q_ref, k_ref, v_ref, gate_ref, s0_ref, y_ref, sout_ref, state_s, *, mm_dtype):
    c = pl.program_id(1)

    @pl.when(c == 0)
    def _():
        state_s[...] = s0_ref[0]

    cl = q_ref.shape[1]
    dk = state_s.shape[1]
    ii = lax.broadcasted_iota(I32, (cl, cl), 0)
    jj = lax.broadcasted_iota(I32, (cl, cl), 1)
    dist = (ii - jj).astype(F32)
    ri = lax.broadcasted_iota(I32, (cl, 1), 0).astype(F32)
    for h in range(H_RET):
        log_g = _retention_log_decay(h)
        sl = slice(h * dk, (h + 1) * dk)
        q = q_ref[0, :, sl]
        k = k_ref[0, :, sl]
        decay = jnp.where(dist >= 0, jnp.exp(jnp.maximum(dist, 0.0) * log_g), 0.0)
        row_decay = jnp.exp((ri + 1.0) * log_g)
        col_decay = jnp.exp((cl - 1.0 - ri) * log_g)
        qm = q.astype(mm_dtype)
        vm = v_ref[0, :, sl].astype(mm_dtype)
        state = state_s[h]
        sc = lax.dot_general(qm, k.astype(mm_dtype), (((1,), (1,)), ((), ())), preferred_element_type=F32) * decay
        o = jnp.dot(sc.astype(mm_dtype), vm, preferred_element_type=F32)
        o = o + jnp.dot(qm, state.astype(mm_dtype), preferred_element_type=F32) * row_decay
        kw = (k * col_decay).astype(mm_dtype)
        upd = lax.dot_general(kw, vm, (((0,), (0,)), ((), ())), preferred_element_type=F32)
        state_s[h] = math.exp(cl * log_g) * state + upd
        g = gate_ref[0, :, sl]
        y_ref[0, :, sl] = (g * jax.nn.sigmoid(g) * _rms(o)).astype(y_ref.dtype)

    @pl.when(c == pl.num_programs(1) - 1)
    def _():
        sout_ref[0] = state_s[...]


def _retention(q, k, v, gate, state0, chunk, y_dtype, mm_dtype):
    b, s, dm = q.shape
    nc = s // chunk
    blk = pl.BlockSpec((1, chunk, dm), lambda bi, c: (bi, c, 0))
    st = pl.BlockSpec((1,) + state0.shape[1:], lambda bi, c: (bi, 0, 0, 0))
    return pl.pallas_call(
        functools.partial(_retention_kernel, mm_dtype=mm_dtype),
        grid=(b, nc),
        in_specs=[blk, blk, blk, blk, st],
        out_specs=[blk, st],
        out_shape=[jax.ShapeDtypeStruct((b, s, dm), y_dtype), jax.ShapeDtypeStruct(state0.shape, F32)],
        scratch_shapes=[pltpu.VMEM(state0.shape[1:], F32)],
        compiler_params=_cparams(("parallel", "arbitrary")),
        name="retention",
    )(q, k, v, gate, state0)


def _mem_kv_kernel(mem_ref, g_ref, w_ref, o_ref):
    xn = _rms(mem_ref[...], g_ref[...]).astype(BF16)
    o_ref[...] = jnp.dot(xn, w_ref[...], preferred_element_type=F32)


def _mem_kv(mem, g, w):
    m, dm = mem.shape
    n = w.shape[1]
    return pl.pallas_call(
        _mem_kv_kernel,
        grid=(n // dm,),
        in_specs=[pl.BlockSpec((m, dm), lambda i: (0, 0)), pl.BlockSpec((1, dm), lambda i: (0, 0)),
                  pl.BlockSpec((dm, dm), lambda i: (0, i))],
        out_specs=pl.BlockSpec((m, dm), lambda i: (0, i)),
        out_shape=jax.ShapeDtypeStruct((m, n), F32),
        compiler_params=_cparams(("parallel",)),
        name="mem_kv",
    )(mem, g, w)


def _mem_attn_kernel(q_ref, mk_ref, mv_ref, o_ref):
    q = q_ref[0].astype(BF16)
    mk = mk_ref[0].astype(BF16)
    mv = mv_ref[0].astype(BF16)
    d = q.shape[-1] // H_MEM
    outs = []
    for h in range(H_MEM):
        sl = slice(h * d, (h + 1) * d)
        s = lax.dot_general(q[:, sl], mk[:, sl], (((1,), (1,)), ((), ())), preferred_element_type=F32)
        m = jnp.max(s, axis=-1, keepdims=True)
        p = jnp.exp(s - m)
        p = p / jnp.sum(p, axis=-1, keepdims=True)
        outs.append(jnp.dot(p.astype(BF16), mv[:, sl], preferred_element_type=F32))
    o_ref[0] = jnp.concatenate(outs, axis=-1).astype(o_ref.dtype)


def _mem_attn(q, mk, mv, y_dtype):
    b, t, dm = q.shape
    tm = _row_tile(t, 512)
    qb = pl.BlockSpec((1, tm, dm), lambda bi, i: (bi, i, 0))
    mb = pl.BlockSpec((1,) + mk.shape[1:], lambda bi, i: (bi, 0, 0))
    return pl.pallas_call(
        _mem_attn_kernel,
        grid=(b, t // tm),
        in_specs=[qb, mb, mb],
        out_specs=qb,
        out_shape=jax.ShapeDtypeStruct((b, t, dm), y_dtype),
        compiler_params=_cparams(("parallel", "parallel")),
        name="mem_attn",
    )(q, mk, mv)


def _merge_kernel(x_ref, yd_ref, yr_ref, ym_ref, gt_ref, wb_ref, wo_ref, g2_ref, rw_ref, rb_ref,
                  h1_ref, hn_ref, lg_ref):
    dm = x_ref.shape[-1]
    mixed = jnp.zeros(x_ref.shape, F32)
    for i, y_ref in enumerate((yd_ref, yr_ref, ym_ref)):
        proj = jnp.dot(y_ref[...].astype(BF16), wb_ref[i], preferred_element_type=F32)
        mixed = mixed + jax.nn.sigmoid(gt_ref[:, i * dm:(i + 1) * dm]) * proj
    h1 = x_ref[...] + jnp.dot(mixed.astype(BF16), wo_ref[...], preferred_element_type=F32)
    h1_ref[...] = h1
    hn = _rms(h1, g2_ref[...])
    hn_ref[...] = hn
    lg_ref[...] = jnp.dot(hn.astype(BF16), rw_ref[...], preferred_element_type=F32) + rb_ref[...]


def _merge(x, yd, yr, ym, gt, wb, wo, g2, rw, rb):
    t, dm = x.shape
    tm = _row_tile(t, 512)
    row = lambda i: (i, 0)
    blk = pl.BlockSpec((tm, dm), row)
    c2 = lambda i: (0, 0)
    return pl.pallas_call(
        _merge_kernel,
        grid=(t // tm,),
        in_specs=[blk, blk, blk, blk, pl.BlockSpec((tm, N_BRANCH * dm), row),
                  pl.BlockSpec(wb.shape, lambda i: (0, 0, 0)), pl.BlockSpec(wo.shape, c2),
                  pl.BlockSpec((1, dm), c2), pl.BlockSpec(rw.shape, c2), pl.BlockSpec(rb.shape, c2)],
        out_specs=[blk, blk, pl.BlockSpec((tm, V7X_LANES), row)],
        out_shape=[jax.ShapeDtypeStruct((t, dm), F32), jax.ShapeDtypeStruct((t, dm), F32),
                   jax.ShapeDtypeStruct((t, V7X_LANES), F32)],
        compiler_params=_cparams(("parallel",)),
        name="merge",
    )(x, yd, yr, ym, gt, wb, wo, g2, rw, rb)


def _route_kernel(lg_ref, idx_ref, gate_ref, rank_ref, cnt_ref, carry_s):
    i = pl.program_id(0)

    @pl.when(i == 0)
    def _():
        carry_s[...] = jnp.zeros(carry_s.shape, F32)

    l = lg_ref[...]
    tm = l.shape[0]
    lane = lax.broadcasted_iota(I32, l.shape, 1)
    vals, idxs, hots = [], [], []
    for _ in range(TOP_K):
        m = jnp.max(l, axis=-1, keepdims=True)
        ik = jnp.min(jnp.where(l == m, lane, V7X_LANES), axis=-1, keepdims=True)
        hot = lane == ik
        vals.append(m)
        idxs.append(ik)
        hots.append(hot)
        l = jnp.where(hot, NEG_INF, l)
    es = [jnp.exp(v - vals[0]) for v in vals]
    den = es[0] + es[1] + es[2] + es[3]
    picked = jnp.zeros(l.shape, F32)
    for hot in hots:
        picked = picked + jnp.where(hot, 1.0, 0.0)
    r = lax.broadcasted_iota(I32, (tm, tm), 0)
    c = lax.broadcasted_iota(I32, (tm, tm), 1)
    before = jnp.where(c < r, 1.0, 0.0).astype(BF16)
    cum = jnp.dot(before, picked.astype(BF16), preferred_element_type=F32) + carry_s[0:1, :]
    idx_o = jnp.zeros(l.shape, I32)
    gate_o = jnp.zeros(l.shape, F32)
    rank_o = jnp.zeros(l.shape, I32)
    for k in range(TOP_K):
        rk = jnp.sum(jnp.where(hots[k], cum, 0.0), axis=-1, keepdims=True).astype(I32)
        idx_o = jnp.where(lane == k, idxs[k], idx_o)
        gate_o = jnp.where(lane == k, es[k] / den, gate_o)
        rank_o = jnp.where(lane == k, rk, rank_o)
    idx_ref[...] = idx_o
    gate_ref[...] = gate_o
    rank_ref[...] = rank_o
    carry_s[...] = carry_s[...] + jnp.sum(picked, axis=0, keepdims=True)
    cnt_ref[...] = carry_s[...]


def _route(logits):
    t = logits.shape[0]
    tm = _row_tile(t, 512)
    row = lambda i: (i, 0)
    blk = pl.BlockSpec((tm, V7X_LANES), row)
    return pl.pallas_call(
        _route_kernel,
        grid=(t // tm,),
        in_specs=[blk],
        out_specs=[blk, blk, blk, pl.BlockSpec((8, V7X_LANES), lambda i: (0, 0))],
        out_shape=[jax.ShapeDtypeStruct((t, V7X_LANES), I32), jax.ShapeDtypeStruct((t, V7X_LANES), F32),
                   jax.ShapeDtypeStruct((t, V7X_LANES), I32), jax.ShapeDtypeStruct((8, V7X_LANES), F32)],
        scratch_shapes=[pltpu.VMEM((8, V7X_LANES), F32)],
        compiler_params=_cparams(("arbitrary",)),
        name="route",
    )(logits)


def _row_copy(src, src_row, dst, dst_row, sem):
    return pltpu.make_async_copy(src.at[pl.ds(src_row, 1)], dst.at[pl.ds(dst_row, 1)], sem)


def _dispatch_kernel(dest_ref, hn_ref, xs_in_ref, xs_ref, sem):
    del xs_in_ref
    tm = hn_ref.shape[0]

    def issue(r, c):
        for k in range(TOP_K):
            _row_copy(hn_ref, r, xs_ref, dest_ref[r * TOP_K + k], sem).start()
        return c

    lax.fori_loop(0, tm, issue, 0, unroll=DMA_ISSUE_UNROLL)
    for k in range(TOP_K):
        pltpu.make_async_copy(hn_ref, xs_ref.at[pl.ds(0, tm)], sem).wait()


def _dispatch(dest_flat, hn, xs):
    t, dm = hn.shape
    tm = _row_tile(t, 128)
    return pl.pallas_call(
        _dispatch_kernel,
        grid=(t // tm,),
        in_specs=[pl.BlockSpec((tm * TOP_K,), lambda i: (i,), memory_space=pltpu.SMEM),
                  pl.BlockSpec((tm, dm), lambda i: (i, 0)),
                  pl.BlockSpec(memory_space=pl.ANY)],
        out_specs=pl.BlockSpec(memory_space=pl.ANY),
        out_shape=jax.ShapeDtypeStruct(xs.shape, xs.dtype),
        scratch_shapes=[pltpu.SemaphoreType.DMA(())],
        input_output_aliases={2: 0},
        compiler_params=_cparams(("arbitrary",)),
        name="moe_dispatch",
    )(dest_flat, hn, xs)


def _expert_kernel(be_ref, nu_ref, xs_ref, w1_ref, b1g_ref, b1l_ref, w2_ref, b2_ref, o_ref, w1g_s, w1l_s, w2_s):
    i = pl.program_id(0)
    e = be_ref[i]
    prev = be_ref[jnp.maximum(i - 1, 0)]

    @pl.when((i == 0) | (e != prev))
    def _():
        cw = 2 * V7X_LANES
        r = lax.broadcasted_iota(I32, (cw, cw), 0)
        c = lax.broadcasted_iota(I32, (cw, cw), 1)
        src_col = jnp.where(c < V7X_LANES, 2 * c, 2 * (c - V7X_LANES) + 1)
        sel = jnp.where(r == src_col, 1.0, 0.0).astype(BF16)
        for j in range(w1_ref.shape[2] // cw):
            t = jnp.dot(w1_ref[0, :, j * cw:(j + 1) * cw].astype(BF16), sel, preferred_element_type=F32)
            w1g_s[:, j * V7X_LANES:(j + 1) * V7X_LANES] = t[:, :V7X_LANES].astype(BF16)
            w1l_s[:, j * V7X_LANES:(j + 1) * V7X_LANES] = t[:, V7X_LANES:].astype(BF16)
        w2_s[...] = w2_ref[0].astype(BF16)

    @pl.when(i < nu_ref[0])
    def _():
        x = xs_ref[...].astype(BF16)
        ug = jnp.dot(x, w1g_s[...], preferred_element_type=F32) + b1g_ref[0]
        ul = jnp.dot(x, w1l_s[...], preferred_element_type=F32) + b1l_ref[0]
        x_glu = jnp.minimum(ug, SWIGLU_LIMIT)
        x_lin = jnp.clip(ul, -SWIGLU_LIMIT, SWIGLU_LIMIT)
        act = x_glu * jax.nn.sigmoid(SWIGLU_ALPHA * x_glu) * (x_lin + 1.0)
        o_ref[...] = jnp.dot(act.astype(BF16), w2_s[...], preferred_element_type=F32) + b2_ref[0]

    @pl.when(i >= nu_ref[0])
    def _():
        o_ref[...] = jnp.zeros(o_ref.shape, F32)


def _experts(block_exp, n_used, xs, w1, b1g, b1l, w2, b2):
    cap, dm = xs.shape
    dff = w2.shape[1]
    n_blocks = cap // MOE_BLOCK
    row_in = lambda i, be, nu: (jnp.minimum(i, nu[0] - 1), 0)
    row_out = lambda i, be, nu: (i, 0)
    wsel = lambda i, be, nu: (be[i], 0, 0)
    grid_spec = pltpu.PrefetchScalarGridSpec(
        num_scalar_prefetch=2,
        grid=(n_blocks,),
        in_specs=[pl.BlockSpec((MOE_BLOCK, dm), row_in),
                  pl.BlockSpec((1, dm, 2 * dff), wsel),
                  pl.BlockSpec((1, 1, dff), wsel), pl.BlockSpec((1, 1, dff), wsel),
                  pl.BlockSpec((1, dff, dm), wsel), pl.BlockSpec((1, 1, dm), wsel)],
        out_specs=pl.BlockSpec((MOE_BLOCK, dm), row_out),
        scratch_shapes=[pltpu.VMEM((dm, dff), BF16), pltpu.VMEM((dm, dff), BF16), pltpu.VMEM((dff, dm), BF16)],
    )
    return pl.pallas_call(
        _expert_kernel,
        grid_spec=grid_spec,
        out_shape=jax.ShapeDtypeStruct((cap, dm), F32),
        compiler_params=_cparams(("arbitrary",)),
        name="moe_experts",
    )(block_exp, n_used, xs, w1, b1g, b1l, w2, b2)


def _combine_kernel(dest_ref, gate_ref, h1_ref, gf_ref, eo_ref, y_ref, buf, sem):
    tm = h1_ref.shape[0]

    def issue(r, c):
        for k in range(TOP_K):
            _row_copy(eo_ref, dest_ref[r * TOP_K + k], buf.at[k], r, sem).start()
        return c

    lax.fori_loop(0, tm, issue, 0, unroll=DMA_ISSUE_UNROLL)
    for k in range(TOP_K):
        pltpu.make_async_copy(eo_ref.at[pl.ds(0, tm)], buf.at[k], sem).wait()
    gate = gate_ref[...]
    y = h1_ref[...]
    for k in range(TOP_K):
        y = y + gate[:, k:k + 1] * buf[k]
    y_ref[...] = _rms(y, gf_ref[...])


def _combine(dest_flat, gate, h1, gf, eo):
    t, dm = h1.shape
    tm = _row_tile(t, 128)
    row = lambda i: (i, 0)
    return pl.pallas_call(
        _combine_kernel,
        grid=(t // tm,),
        in_specs=[pl.BlockSpec((tm * TOP_K,), lambda i: (i,), memory_space=pltpu.SMEM),
                  pl.BlockSpec((tm, V7X_LANES), row), pl.BlockSpec((tm, dm), row),
                  pl.BlockSpec((1, dm), lambda i: (0, 0)), pl.BlockSpec(memory_space=pl.ANY)],
        out_specs=pl.BlockSpec((tm, dm), row),
        out_shape=jax.ShapeDtypeStruct((t, dm), F32),
        scratch_shapes=[pltpu.VMEM((TOP_K, tm, dm), F32), pltpu.SemaphoreType.DMA(())],
        compiler_params=_cparams(("arbitrary",)),
        name="moe_combine",
    )(dest_flat, gate, h1, gf, eo)


def _moe_and_final_norm(parts, router_b_unused, w1, b1, w2, b2, normf_g):
    del router_b_unused
    sizes = [p[0].shape[0] for p in parts]
    t = sum(sizes)
    dm = parts[0][0].shape[1]
    logits = jnp.concatenate([p[2] for p in parts], axis=0)
    idx, gate, rank, counts = _route(logits)
    cnt = counts[0, :N_EXPERTS].astype(I32)
    padded = ((cnt + MOE_BLOCK - 1) // MOE_BLOCK) * MOE_BLOCK
    pad_ends = jnp.cumsum(padded)
    pad_starts = pad_ends - padded
    n_blocks = -(-(t * TOP_K + N_EXPERTS * (MOE_BLOCK - 1)) // MOE_BLOCK)
    cap = n_blocks * MOE_BLOCK
    block_start = jnp.arange(n_blocks, dtype=I32) * MOE_BLOCK
    block_exp = jnp.minimum(jnp.sum(pad_ends[None, :] <= block_start[:, None], axis=1), N_EXPERTS - 1).astype(I32)
    n_used = (pad_ends[-1:] // MOE_BLOCK).astype(I32)
    dest = (pad_starts[idx[:, :TOP_K]] + rank[:, :TOP_K]).astype(I32).reshape(-1)

    xs = jnp.zeros((cap, dm), F32)
    off = 0
    for (h1, hn, _), n in zip(parts, sizes):
        xs = _dispatch(dest[off * TOP_K:(off + n) * TOP_K], hn, xs)
        off += n
    b1g = b1[:, None, 0::2]
    b1l = b1[:, None, 1::2]
    eo = _experts(block_exp, n_used, xs, w1, b1g, b1l, w2, b2[:, None, :])
    outs = []
    off = 0
    for (h1, hn, _), n in zip(parts, sizes):
        outs.append(_combine(dest[off * TOP_K:(off + n) * TOP_K], gate[off:off + n], h1, normf_g, eo))
        off += n
    return outs


def _rope_tables(pos, half):
    inv = ROPE_BASE ** (-jnp.arange(half, dtype=F32) / half)
    ang = pos.astype(F32)[:, None] * inv[None, :]
    return jnp.cos(ang), jnp.sin(ang)


def kernel(x_prompt, x_sample, cache_k, cache_v, state_ret, cache_mem_k, cache_mem_v, page_table, mem_prompt,
           norm1_g, w_in, lambda_q1, lambda_k1, lambda_q2, lambda_k2, subln_g, norm_mem_g, w_mem_kv, w_branch,
           w_out, norm2_g, router_w, router_b, w1, b1, w2, b2, normf_g):
    b, s, dm = x_prompt.shape
    db, t, _ = x_sample.shape
    depth = w_in.shape[0]
    assert depth == 1, "final norm is fused into the last layer's MoE combine; one layer supported"
    assert b == 1
    page = cache_k.shape[2]
    past_len = page_table.shape[1] * page
    dk_ret = dm // H_RET
    n_mem = mem_prompt.shape[1]

    cos_p, sin_p = _rope_tables(jnp.arange(s), dk_ret // 2)
    cos_s, sin_s = _rope_tables(past_len + jnp.arange(t), dk_ret // 2)
    cos_s = jnp.tile(cos_s, (db, 1))
    sin_s = jnp.tile(sin_s, (db, 1))

    l = 0
    lam_init = 0.8 - 0.6 * math.exp(-0.3 * l)
    lam4 = jnp.stack([lambda_q1[l], lambda_k1[l], lambda_q2[l], lambda_k2[l]]).astype(F32)
    g1 = norm1_g[l][None, :]
    sub_g = subln_g[l][None, :]
    d3 = 3 * dm
    w_l = w_in[l]
    w_diff = w_l[:, :d3].astype(BF16)
    w_ret = w_l[:, d3:d3 + 4 * dm].astype(BF16)
    w_mg = w_l[:, d3 + 4 * dm:].astype(BF16)
    wb = w_branch[l].astype(BF16)
    wo = w_out[l].astype(BF16)
    g2 = norm2_g[l][None, :]
    rw = jnp.zeros((dm, V7X_LANES), F32).at[:, :N_EXPERTS].set(router_w[l]).astype(BF16)
    rb = jnp.full((1, V7X_LANES), NEG_INF, F32).at[0, :N_EXPERTS].set(router_b[l])

    xp = x_prompt.reshape(s, dm)
    q_d, k_f, v_f, k_b, v_t = _proj_diff(xp, g1, w_diff, BF16)
    y_d = _diff_attn_prompt(q_d, k_b, v_t, lam4, subln_g[l][:, None], lam_init)
    rq, rk, rv, rg = _proj_ret(xp, g1, w_ret, cos_p, sin_p)
    to3 = lambda a: a.reshape(1, s, dm)
    y_r, st_p = _retention(to3(rq), to3(rk), to3(rv), to3(rg), jnp.zeros((1, H_RET, dk_ret, dk_ret), F32),
                           chunk=_row_tile(s, RET_CHUNK), y_dtype=BF16, mm_dtype=BF16)
    mq, gt = _proj_mem_gate(xp, g1, w_mg, BF16)
    mkv = _mem_kv(mem_prompt.reshape(n_mem, dm), norm_mem_g[l][None, :], w_mem_kv[l].astype(BF16))
    mk, mv = mkv[:, :dm], mkv[:, dm:]
    y_m = _mem_attn(mq[None], mk[None], mv[None], BF16)
    part_p = _merge(xp, y_d, y_r.reshape(s, dm), y_m.reshape(s, dm), gt, wb, wo, g2, rw, rb)

    ns = db * t
    xs_ = x_sample.reshape(ns, dm)
    q_s, ks_f, vs_f, _, _ = _proj_diff(xs_, g1, w_diff, F32)
    k_s5 = ks_f.reshape(db, t, H_DIFF, D_V)
    v_s5 = vs_f.reshape(db, t, H_DIFF, D_V)
    y_ds = _diff_attn_sample(page_table, q_s.reshape(db, t, dm), k_s5, v_s5, cache_k, cache_v, l,
                             lam4, sub_g, lam_init)
    rq, rk, rv, rg = _proj_ret(xs_, g1, w_ret, cos_s, sin_s)
    tos = lambda a: a.reshape(db, t, dm)
    y_rs, st_s = _retention(tos(rq), tos(rk), tos(rv), tos(rg), state_ret[l].astype(F32),
                            chunk=t, y_dtype=F32, mm_dtype=F32)
    mq_s, gt_s = _proj_mem_gate(xs_, g1, w_mg, F32)
    y_ms = _mem_attn(mq_s.reshape(db, t, dm), cache_mem_k[l].reshape(db, n_mem, dm),
                     cache_mem_v[l].reshape(db, n_mem, dm), F32)
    part_s = _merge(xs_, y_ds.reshape(ns, dm), y_rs.reshape(ns, dm), y_ms.reshape(ns, dm), gt_s, wb, wo, g2, rw, rb)

    y_p, y_s = _moe_and_final_norm([part_p, part_s], None, w1[l], b1[l], w2[l], b2[l], normf_g[None, :])

    return (y_p.reshape(b, s, dm), y_s.reshape(db, t, dm),
            k_f.reshape(1, b, s, H_DIFF, D_V), v_f.reshape(1, b, s, H_DIFF, D_V),
            st_p.reshape(1, b, H_RET, dk_ret, dk_ret),
            mk.reshape(1, b, n_mem, H_MEM, dm // H_MEM), mv.reshape(1, b, n_mem, H_MEM, dm // H_MEM),
            k_s5[None], v_s5[None], st_s[None].astype(state_ret.dtype))
```

```python
import functools
import math

import jax
import jax.numpy as jnp
import numpy as np
from jax import lax
from jax.experimental import pallas as pl
from jax.experimental.pallas import tpu as pltpu

F32 = jnp.float32
BF16 = jnp.bfloat16
I32 = jnp.int32

H_DIFF = 8
D_QK = 64
D_V = 128
H_RET = 4
H_MEM = 4
N_BRANCH = 3
N_EXPERTS = 32
TOP_K = 4
SWIGLU_LIMIT = 7.0
SWIGLU_ALPHA = 1.702
ROPE_BASE = 10000.0
RMS_EPS = 1e-6

V7X_LANES = 128
V7X_VMEM_LIMIT_BYTES = 56 * 1024 * 1024

MOE_BLOCK = 256
DMA_ISSUE_UNROLL = 16
SAMPLE_PAGES_PER_STEP = 16
SAMPLE_PAGES_PER_UPDATE = 4
RET_CHUNK = 256
NEG_INF = float("-inf")
DIFF_Q_SCALE = (D_QK ** -0.5) * math.log2(math.e)


def _cparams(sem):
    return pltpu.CompilerParams(dimension_semantics=sem, vmem_limit_bytes=V7X_VMEM_LIMIT_BYTES)


def _rms(x, g=None):
    y = x * lax.rsqrt(jnp.mean(x * x, axis=-1, keepdims=True) + RMS_EPS)
    return y if g is None else y * g


def _row_tile(n, pref):
    t = min(n, pref)
    while n % t:
        t //= 2
    return t


def _proj_diff_kernel(x_ref, g_ref, w_ref, q_ref, kf_ref, vf_ref, kb_ref, vt_ref):
    xn = _rms(x_ref[...], g_ref[...]).astype(BF16)
    d = kf_ref.shape[-1]
    q = jnp.dot(xn, w_ref[:, 0:d], preferred_element_type=F32)
    q_ref[...] = (q * DIFF_Q_SCALE).astype(q_ref.dtype)
    k = jnp.dot(xn, w_ref[:, d:2 * d], preferred_element_type=F32)
    kf_ref[...] = k
    kb_ref[...] = k.astype(BF16)
    v = jnp.dot(xn, w_ref[:, 2 * d:3 * d], preferred_element_type=F32)
    vf_ref[...] = v
    vt_ref[...] = v.T.astype(BF16)


def _proj_diff(x, g, w, q_dtype):
    t, dm = x.shape
    tm = _row_tile(t, 512)
    row = lambda i: (i, 0)
    blk = pl.BlockSpec((tm, dm), row)
    return pl.pallas_call(
        _proj_diff_kernel,
        grid=(t // tm,),
        in_specs=[blk, pl.BlockSpec((1, dm), lambda i: (0, 0)), pl.BlockSpec(w.shape, lambda i: (0, 0))],
        out_specs=[blk] * 4 + [pl.BlockSpec((dm, tm), lambda i: (0, i))],
        out_shape=[jax.ShapeDtypeStruct((t, dm), q_dtype), jax.ShapeDtypeStruct((t, dm), F32),
                   jax.ShapeDtypeStruct((t, dm), F32), jax.ShapeDtypeStruct((t, dm), BF16),
                   jax.ShapeDtypeStruct((dm, t), BF16)],
        compiler_params=_cparams(("parallel",)),
        name="proj_diff",
    )(x, g, w)


def _proj_ret_kernel(x_ref, g_ref, w_ref, cos_ref, sin_ref, q_ref, k_ref, v_ref, gate_ref):
    xn = _rms(x_ref[...], g_ref[...]).astype(BF16)
    d = q_ref.shape[-1]
    dk = d // H_RET
    half = dk // 2
    cos = cos_ref[...]
    sin = sin_ref[...]

    def rope(u, scale):
        outs = []
        for h in range(H_RET):
            x1 = u[:, h * dk:h * dk + half]
            x2 = u[:, h * dk + half:(h + 1) * dk]
            outs.append((x1 * cos - x2 * sin) * scale)
            outs.append((x1 * sin + x2 * cos) * scale)
        return jnp.concatenate(outs, axis=-1)

    q_ref[...] = rope(jnp.dot(xn, w_ref[:, 0:d], preferred_element_type=F32), 1.0)
    k_ref[...] = rope(jnp.dot(xn, w_ref[:, d:2 * d], preferred_element_type=F32), dk ** -0.5)
    v_ref[...] = jnp.dot(xn, w_ref[:, 2 * d:3 * d], preferred_element_type=F32)
    gate_ref[...] = jnp.dot(xn, w_ref[:, 3 * d:4 * d], preferred_element_type=F32)


def _proj_ret(x, g, w, cos, sin):
    t, dm = x.shape
    tm = _row_tile(t, 512)
    row = lambda i: (i, 0)
    blk = pl.BlockSpec((tm, dm), row)
    tab = pl.BlockSpec((tm, cos.shape[1]), row)
    return pl.pallas_call(
        _proj_ret_kernel,
        grid=(t // tm,),
        in_specs=[blk, pl.BlockSpec((1, dm), lambda i: (0, 0)), pl.BlockSpec(w.shape, lambda i: (0, 0)), tab, tab],
        out_specs=[blk] * 4,
        out_shape=[jax.ShapeDtypeStruct((t, dm), F32)] * 4,
        compiler_params=_cparams(("parallel",)),
        name="proj_ret",
    )(x, g, w, cos, sin)


def _proj_mem_gate_kernel(x_ref, g_ref, w_ref, mq_ref, gt_ref):
    xn = _rms(x_ref[...], g_ref[...]).astype(BF16)
    d = mq_ref.shape[-1]
    mq = jnp.dot(xn, w_ref[:, 0:d], preferred_element_type=F32)
    mq_ref[...] = (mq * ((d // H_MEM) ** -0.5)).astype(mq_ref.dtype)
    gt_ref[...] = jnp.dot(xn, w_ref[:, d:], preferred_element_type=F32)


def _proj_mem_gate(x, g, w, mq_dtype):
    t, dm = x.shape
    tm = _row_tile(t, 512)
    row = lambda i: (i, 0)
    return pl.pallas_call(
        _proj_mem_gate_kernel,
        grid=(t // tm,),
        in_specs=[pl.BlockSpec((tm, dm), row), pl.BlockSpec((1, dm), lambda i: (0, 0)),
                  pl.BlockSpec(w.shape, lambda i: (0, 0))],
        out_specs=[pl.BlockSpec((tm, dm), row), pl.BlockSpec((tm, N_BRANCH * dm), row)],
        out_shape=[jax.ShapeDtypeStruct((t, dm), mq_dtype), jax.ShapeDtypeStruct((t, N_BRANCH * dm), F32)],
        compiler_params=_cparams(("parallel",)),
        name="proj_mem_gate",
    )(x, g, w)


def _lambda_value(lam_ref, lam_init):
    t = lam_ref[...]
    a = jnp.sum(t[0:1] * t[1:2], axis=-1, keepdims=True)
    b = jnp.sum(t[2:3] * t[3:4], axis=-1, keepdims=True)
    return jnp.exp(a) - jnp.exp(b) + lam_init


def _diff_attn_prompt_kernel(q_ref, k_ref, vt_ref, lam_ref, g_ref, o_ref, m_s, l_s, acc_s, sa_s, sb_s, mba_s, mbb_s,
                             *, tq, tk, rc, lam_init):
    qi = pl.program_id(1)
    q = q_ref[...]
    lane = lax.broadcasted_iota(I32, q.shape, 1)
    zero = jnp.zeros_like(q)
    qz = jnp.concatenate([jnp.where(lane < D_QK, q, zero), jnp.where(lane >= D_QK, q, zero)], axis=0)
    m_s[...] = jnp.full(m_s.shape, NEG_INF, F32)
    l_s[...] = jnp.zeros(l_s.shape, F32)
    acc_s[...] = jnp.zeros(acc_s.shape, F32)
    n_full = (qi * tq) // tk

    def scores(j, masked, s_buf, mb_buf):
        ks = k_ref[pl.ds(pl.multiple_of(j * tk, tk), tk), :]
        s = lax.dot_general(ks, qz, (((1,), (1,)), ((), ())), preferred_element_type=F32)
        if masked:
            kpos = lax.broadcasted_iota(I32, s.shape, 0) + j * tk
            col = lax.broadcasted_iota(I32, s.shape, 1)
            qpos = jnp.where(col >= tq, col - tq, col) + qi * tq
            s = jnp.where(kpos <= qpos, s, NEG_INF)
        s_buf[...] = s
        mb_buf[...] = jnp.max(s, axis=0, keepdims=True)

    def absorb(s_buf, mb_buf, vblock):
        m_old = m_s[...]
        m_new = jnp.maximum(m_old, mb_buf[...])
        alpha = jnp.exp2(m_old - m_new)
        part = jnp.zeros((8, 2 * tq), F32)
        vstart = pl.multiple_of(vblock * tk, tk)
        pv = None
        for r in range(tk // rc):
            p = jnp.exp2(s_buf[r * rc:(r + 1) * rc, :] - m_new)
            part = part + jnp.sum(p.reshape(rc // 8, 8, 2 * tq), axis=0)
            vt = vt_ref[:, pl.ds(vstart + r * rc, rc)]
            d = jnp.dot(vt, p.astype(BF16), preferred_element_type=F32)
            pv = d if pv is None else pv + d
        l_s[...] = alpha * l_s[...] + jnp.sum(part, axis=0, keepdims=True)
        acc_s[...] = alpha * acc_s[...] + pv
        m_s[...] = m_new

    scores(n_full, True, sa_s, mba_s)
    n_pairs = n_full // 2

    def body(i, c):
        scores(2 * i, False, sb_s, mbb_s)
        absorb(sa_s, mba_s, jnp.where(i == 0, n_full, 2 * i - 1))
        scores(2 * i + 1, False, sa_s, mba_s)
        absorb(sb_s, mbb_s, 2 * i)
        return c

    lax.fori_loop(0, n_pairs, body, 0)
    pending = jnp.where(n_pairs == 0, n_full, 2 * n_pairs - 1)

    @pl.when(n_full % 2 == 1)
    def _():
        scores(n_full - 1, False, sb_s, mbb_s)
        absorb(sa_s, mba_s, pending)
        absorb(sb_s, mbb_s, n_full - 1)

    @pl.when(n_full % 2 == 0)
    def _():
        absorb(sa_s, mba_s, pending)

    o = acc_s[...] / l_s[...]
    lam = _lambda_value(lam_ref, lam_init)
    od = o[:, :tq] - lam * o[:, tq:]
    y = od * lax.rsqrt(jnp.mean(od * od, axis=0, keepdims=True) + RMS_EPS) * g_ref[...] * (1.0 - lam_init)
    o_ref[...] = y.T.astype(o_ref.dtype)


def _diff_attn_prompt(q, k, vt, lam4, subln_g_col, lam_init):
    s, dm = q.shape
    tq = _row_tile(s, 512)
    tk = _row_tile(s, 1024)
    rc = _row_tile(tk, 256)
    kern = functools.partial(_diff_attn_prompt_kernel, tq=tq, tk=tk, rc=rc, lam_init=lam_init)
    row_stat = pltpu.VMEM((1, 2 * tq), F32)
    score_buf = pltpu.VMEM((tk, 2 * tq), F32)
    return pl.pallas_call(
        kern,
        grid=(H_DIFF, s // tq),
        in_specs=[pl.BlockSpec((tq, D_V), lambda h, i: (i, h)),
                  pl.BlockSpec((s, D_V), lambda h, i: (0, h)),
                  pl.BlockSpec((D_V, s), lambda h, i: (h, 0)),
                  pl.BlockSpec(lam4.shape, lambda h, i: (0, 0)),
                  pl.BlockSpec((D_V, 1), lambda h, i: (0, 0))],
        out_specs=pl.BlockSpec((tq, D_V), lambda h, i: (i, h)),
        out_shape=jax.ShapeDtypeStruct((s, dm), BF16),
        scratch_shapes=[row_stat, row_stat, pltpu.VMEM((D_V, 2 * tq), F32),
                        score_buf, score_buf, row_stat, row_stat],
        compiler_params=_cparams(("parallel", "parallel")),
        name="diff_attn_prompt",
    )(q, k, vt, lam4, subln_g_col)


def _diff_attn_sample_kernel(pt_ref, q_ref, kn_ref, vn_ref, lam_ref, g_ref, *rest, pps, lam_init):
    k_refs = rest[:pps]
    v_refs = rest[pps:2 * pps]
    o_ref, qmat_s, bias_s, m_s, l_s, acc_s, s_s = rest[2 * pps:]
    j = pl.program_id(1)
    nq = q_ref.shape[1]
    pc = bias_s.shape[1]

    @pl.when(j == 0)
    def _():
        qb = q_ref[0]
        lane = lax.broadcasted_iota(I32, (nq, D_V), 1)
        parts = []
        for h in range(H_DIFF):
            blk = qb[:, h * D_V:(h + 1) * D_V]
            parts.append(jnp.where(lane < D_QK, blk, 0.0))
            parts.append(jnp.where(lane >= D_QK, blk, 0.0))
        qmat_s[...] = jnp.concatenate(parts, axis=0).astype(BF16)
        r = lax.broadcasted_iota(I32, bias_s.shape, 0)
        c = lax.broadcasted_iota(I32, bias_s.shape, 1)
        bias_s[...] = jnp.where((c % H_DIFF) == (r // (2 * nq)), 0.0, NEG_INF)
        m_s[...] = jnp.full(m_s.shape, NEG_INF, F32)
        l_s[...] = jnp.zeros(l_s.shape, F32)
        acc_s[...] = jnp.zeros(acc_s.shape, F32)

    def flat(page):
        return page.reshape(page.shape[0] * page.shape[1], page.shape[2]).astype(BF16)

    qmat = qmat_s[...]
    grp = s_s.shape[1] // pc
    for g0 in range(0, pps, grp):
        mb = None
        for r in range(grp):
            s = lax.dot_general(qmat, flat(k_refs[g0 + r][0, 0]), (((1,), (1,)), ((), ())),
                                preferred_element_type=F32) + bias_s[...]
            s_s[:, r * pc:(r + 1) * pc] = s
            mr = jnp.max(s, axis=-1, keepdims=True)
            mb = mr if mb is None else jnp.maximum(mb, mr)
        m_old = m_s[...]
        m_new = jnp.maximum(m_old, mb)
        alpha = jnp.exp2(m_old - m_new)
        lsum = None
        pv = None
        for r in range(grp):
            p = jnp.exp2(s_s[:, r * pc:(r + 1) * pc] - m_new)
            ls = jnp.sum(p, axis=-1, keepdims=True)
            lsum = ls if lsum is None else lsum + ls
            d = jnp.dot(p.astype(BF16), flat(v_refs[g0 + r][0, 0]), preferred_element_type=F32)
            pv = d if pv is None else pv + d
        l_s[...] = alpha * l_s[...] + lsum
        acc_s[...] = alpha * acc_s[...] + pv
        m_s[...] = m_new

    @pl.when(j == pl.num_programs(1) - 1)
    def _():
        s = lax.dot_general(qmat, flat(kn_ref[0]), (((1,), (1,)), ((), ())), preferred_element_type=F32)
        r = lax.broadcasted_iota(I32, s.shape, 0)
        c = lax.broadcasted_iota(I32, s.shape, 1)
        keep = ((c % H_DIFF) == (r // (2 * nq))) & ((c // H_DIFF) <= (r % nq))
        s = jnp.where(keep, s, NEG_INF)
        m_o = m_s[...]
        m_n = jnp.maximum(m_o, jnp.max(s, axis=-1, keepdims=True))
        al = jnp.exp2(m_o - m_n)
        p = jnp.exp2(s - m_n)
        l = al * l_s[...] + jnp.sum(p, axis=-1, keepdims=True)
        acc = al * acc_s[...] + jnp.dot(p.astype(BF16), flat(vn_ref[0]), preferred_element_type=F32)
        o = acc / l
        lam = _lambda_value(lam_ref, lam_init)
        g = g_ref[...]
        outs = []
        for h in range(H_DIFF):
            base = h * 2 * nq
            od = o[base:base + nq] - lam * o[base + nq:base + 2 * nq]
            outs.append(_rms(od, g) * (1.0 - lam_init))
        o_ref[0] = jnp.concatenate(outs, axis=-1)


def _diff_attn_sample(page_table, q, k_new, v_new, cache_k, cache_v, layer, lam4, subln_g, lam_init,
                      pps=SAMPLE_PAGES_PER_STEP, grp=SAMPLE_PAGES_PER_UPDATE):
    b, nq, dm = q.shape
    n_pages = page_table.shape[1]
    while n_pages % pps:
        pps //= 2
    grp = min(grp, pps)
    page_shape = (1, 1) + cache_k.shape[2:]
    page_specs = [pl.BlockSpec(page_shape, functools.partial(
        lambda bi, j, pt, r: (layer, pt[bi, j * pps + r], 0, 0, 0), r=r)) for r in range(pps)]
    rows = H_DIFF * 2 * nq
    cols = cache_k.shape[2] * cache_k.shape[3]
    kern = functools.partial(_diff_attn_sample_kernel, pps=pps, lam_init=lam_init)
    per_seq3 = lambda bi, j, pt: (bi, 0, 0)
    per_seq4 = lambda bi, j, pt: (bi, 0, 0, 0)
    const2 = lambda bi, j, pt: (0, 0)
    grid_spec = pltpu.PrefetchScalarGridSpec(
        num_scalar_prefetch=1,
        grid=(b, n_pages // pps),
        in_specs=[pl.BlockSpec((1, nq, dm), per_seq3),
                  pl.BlockSpec((1,) + k_new.shape[1:], per_seq4),
                  pl.BlockSpec((1,) + v_new.shape[1:], per_seq4),
                  pl.BlockSpec(lam4.shape, const2),
                  pl.BlockSpec((1, D_V), const2)] + page_specs + page_specs,
        out_specs=pl.BlockSpec((1, nq, dm), per_seq3),
        scratch_shapes=[pltpu.VMEM((rows, D_V), BF16), pltpu.VMEM((rows, cols), F32),
                        pltpu.VMEM((rows, 1), F32), pltpu.VMEM((rows, 1), F32), pltpu.VMEM((rows, D_V), F32),
                        pltpu.VMEM((rows, grp * cols), F32)],
    )
    return pl.pallas_call(
        kern,
        grid_spec=grid_spec,
        out_shape=jax.ShapeDtypeStruct((b, nq, dm), F32),
        compiler_params=_cparams(("parallel", "arbitrary")),
        name="diff_attn_sample",
    )(page_table, q, k_new, v_new, lam4, subln_g, *([cache_k] * pps), *([cache_v] * pps))


def _retention_log_decay(h):
    return float(np.log(np.float32(1.0) - np.exp2(np.float32(-5.0 - h))))


def _retention_kernel(q_ref, k_ref, v_ref, gate_ref, s0_ref, y_ref, sout_ref, state_s, *, mm_dtype):
    c = pl.program_id(1)

    @pl.when(c == 0)
    def _():
        state_s[...] = s0_ref[0]

    cl = q_ref.shape[1]
    dk = state_s.shape[1]
    ii = lax.broadcasted_iota(I32, (cl, cl), 0)
    jj = lax.broadcasted_iota(I32, (cl, cl), 1)
    dist = (ii - jj).astype(F32)
    ri = lax.broadcasted_iota(I32, (cl, 1), 0).astype(F32)
    for h in range(H_RET):
        log_g = _retention_log_decay(h)
        sl = slice(h * dk, (h + 1) * dk)
        q = q_ref[0, :, sl]
        k = k_ref[0, :, sl]
        decay = jnp.where(dist >= 0, jnp.exp(jnp.maximum(dist, 0.0) * log_g), 0.0)
        row_decay = jnp.exp((ri + 1.0) * log_g)
        col_decay = jnp.exp((cl - 1.0 - ri) * log_g)
        qm = q.astype(mm_dtype)
        vm = v_ref[0, :, sl].astype(mm_dtype)
        state = state_s[h]
        sc = lax.dot_general(qm, k.astype(mm_dtype), (((1,), (1,)), ((), ())), preferred_element_type=F32) * decay
        o = jnp.dot(sc.astype(mm_dtype), vm, preferred_element_type=F32)
        o = o + jnp.dot(qm, state.astype(mm_dtype), preferred_element_type=F32) * row_decay
        kw = (k * col_decay).astype(mm_dtype)
        upd = lax.dot_general(kw, vm, (((0,), (0,)), ((), ())), preferred_element_type=F32)
        state_s[h] = math.exp(cl * log_g) * state + upd
        g = gate_ref[0, :, sl]
        y_ref[0, :, sl] = (g * jax.nn.sigmoid(g) * _rms(o)).astype(y_ref.dtype)

    @pl.when(c == pl.num_programs(1) - 1)
    def _():
        sout_ref[0] = state_s[...]


def _retention(q, k, v, gate, state0, chunk, y_dtype, mm_dtype):
    b, s, dm = q.shape
    nc = s // chunk
    blk = pl.BlockSpec((1, chunk, dm), lambda bi, c: (bi, c, 0))
    st = pl.BlockSpec((1,) + state0.shape[1:], lambda bi, c: (bi, 0, 0, 0))
    return pl.pallas_call(
        functools.partial(_retention_kernel, mm_dtype=mm_dtype),
        grid=(b, nc),
        in_specs=[blk, blk, blk, blk, st],
        out_specs=[blk, st],
        out_shape=[jax.ShapeDtypeStruct((b, s, dm), y_dtype), jax.ShapeDtypeStruct(state0.shape, F32)],
        scratch_shapes=[pltpu.VMEM(state0.shape[1:], F32)],
        compiler_params=_cparams(("parallel", "arbitrary")),
        name="retention",
    )(q, k, v, gate, state0)


def _mem_kv_kernel(mem_ref, g_ref, w_ref, o_ref):
    xn = _rms(mem_ref[...], g_ref[...]).astype(BF16)
    o_ref[...] = jnp.dot(xn, w_ref[...], preferred_element_type=F32)


def _mem_kv(mem, g, w):
    m, dm = mem.shape
    n = w.shape[1]
    return pl.pallas_call(
        _mem_kv_kernel,
        grid=(n // dm,),
        in_specs=[pl.BlockSpec((m, dm), lambda i: (0, 0)), pl.BlockSpec((1, dm), lambda i: (0, 0)),
                  pl.BlockSpec((dm, dm), lambda i: (0, i))],
        out_specs=pl.BlockSpec((m, dm), lambda i: (0, i)),
        out_shape=jax.ShapeDtypeStruct((m, n), F32),
        compiler_params=_cparams(("parallel",)),
        name="mem_kv",
    )(mem, g, w)


def _mem_attn_kernel(q_ref, mk_ref, mv_ref, o_ref):
    q = q_ref[0].astype(BF16)
    mk = mk_ref[0].astype(BF16)
    mv = mv_ref[0].astype(BF16)
    d = q.shape[-1] // H_MEM
    outs = []
    for h in range(H_MEM):
        sl = slice(h * d, (h + 1) * d)
        s = lax.dot_general(q[:, sl], mk[:, sl], (((1,), (1,)), ((), ())), preferred_element_type=F32)
        m = jnp.max(s, axis=-1, keepdims=True)
        p = jnp.exp(s - m)
        p = p / jnp.sum(p, axis=-1, keepdims=True)
        outs.append(jnp.dot(p.astype(BF16), mv[:, sl], preferred_element_type=F32))
    o_ref[0] = jnp.concatenate(outs, axis=-1).astype(o_ref.dtype)


def _mem_attn(q, mk, mv, y_dtype):
    b, t, dm = q.shape
    tm = _row_tile(t, 512)
    qb = pl.BlockSpec((1, tm, dm), lambda bi, i: (bi, i, 0))
    mb = pl.BlockSpec((1,) + mk.shape[1:], lambda bi, i: (bi, 0, 0))
    return pl.pallas_call(
        _mem_attn_kernel,
        grid=(b, t // tm),
        in_specs=[qb, mb, mb],
        out_specs=qb,
        out_shape=jax.ShapeDtypeStruct((b, t, dm), y_dtype),
        compiler_params=_cparams(("parallel", "parallel")),
        name="mem_attn",
    )(q, mk, mv)


def _merge_kernel(x_ref, yd_ref, yr_ref, ym_ref, gt_ref, wb_ref, wo_ref, g2_ref, rw_ref, rb_ref,
                  h1_ref, hn_ref, lg_ref):
    dm = x_ref.shape[-1]
    mixed = jnp.zeros(x_ref.shape, F32)
    for i, y_ref in enumerate((yd_ref, yr_ref, ym_ref)):
        proj = jnp.dot(y_ref[...].astype(BF16), wb_ref[i], preferred_element_type=F32)
        mixed = mixed + jax.nn.sigmoid(gt_ref[:, i * dm:(i + 1) * dm]) * proj
    h1 = x_ref[...] + jnp.dot(mixed.astype(BF16), wo_ref[...], preferred_element_type=F32)
    h1_ref[...] = h1
    hn = _rms(h1, g2_ref[...])
    hn_ref[...] = hn
    lg_ref[...] = jnp.dot(hn.astype(BF16), rw_ref[...], preferred_element_type=F32) + rb_ref[...]


def _merge(x, yd, yr, ym, gt, wb, wo, g2, rw, rb):
    t, dm = x.shape
    tm = _row_tile(t, 512)
    row = lambda i: (i, 0)
    blk = pl.BlockSpec((tm, dm), row)
    c2 = lambda i: (0, 0)
    return pl.pallas_call(
        _merge_kernel,
        grid=(t // tm,),
        in_specs=[blk, blk, blk, blk, pl.BlockSpec((tm, N_BRANCH * dm), row),
                  pl.BlockSpec(wb.shape, lambda i: (0, 0, 0)), pl.BlockSpec(wo.shape, c2),
                  pl.BlockSpec((1, dm), c2), pl.BlockSpec(rw.shape, c2), pl.BlockSpec(rb.shape, c2)],
        out_specs=[blk, blk, pl.BlockSpec((tm, V7X_LANES), row)],
        out_shape=[jax.ShapeDtypeStruct((t, dm), F32), jax.ShapeDtypeStruct((t, dm), F32),
                   jax.ShapeDtypeStruct((t, V7X_LANES), F32)],
        compiler_params=_cparams(("parallel",)),
        name="merge",
    )(x, yd, yr, ym, gt, wb, wo, g2, rw, rb)


def _route_kernel(lg_ref, idx_ref, gate_ref, rank_ref, cnt_ref, carry_s):
    i = pl.program_id(0)

    @pl.when(i == 0)
    def _():
        carry_s[...] = jnp.zeros(carry_s.shape, F32)

    l = lg_ref[...]
    tm = l.shape[0]
    lane = lax.broadcasted_iota(I32, l.shape, 1)
    vals, idxs, hots = [], [], []
    for _ in range(TOP_K):
        m = jnp.max(l, axis=-1, keepdims=True)
        ik = jnp.min(jnp.where(l == m, lane, V7X_LANES), axis=-1, keepdims=True)
        hot = lane == ik
        vals.append(m)
        idxs.append(ik)
        hots.append(hot)
        l = jnp.where(hot, NEG_INF, l)
    es = [jnp.exp(v - vals[0]) for v in vals]
    den = es[0] + es[1] + es[2] + es[3]
    picked = jnp.zeros(l.shape, F32)
    for hot in hots:
        picked = picked + jnp.where(hot, 1.0, 0.0)
    r = lax.broadcasted_iota(I32, (tm, tm), 0)
    c = lax.broadcasted_iota(I32, (tm, tm), 1)
    before = jnp.where(c < r, 1.0, 0.0).astype(BF16)
    cum = jnp.dot(before, picked.astype(BF16), preferred_element_type=F32) + carry_s[0:1, :]
    idx_o = jnp.zeros(l.shape, I32)
    gate_o = jnp.zeros(l.shape, F32)
    rank_o = jnp.zeros(l.shape, I32)
    for k in range(TOP_K):
        rk = jnp.sum(jnp.where(hots[k], cum, 0.0), axis=-1, keepdims=True).astype(I32)
        idx_o = jnp.where(lane == k, idxs[k], idx_o)
        gate_o = jnp.where(lane == k, es[k] / den, gate_o)
        rank_o = jnp.where(lane == k, rk, rank_o)
    idx_ref[...] = idx_o
    gate_ref[...] = gate_o
    rank_ref[...] = rank_o
    carry_s[...] = carry_s[...] + jnp.sum(picked, axis=0, keepdims=True)
    cnt_ref[...] = carry_s[...]


def _route(logits):
    t = logits.shape[0]
    tm = _row_tile(t, 512)
    row = lambda i: (i, 0)
    blk = pl.BlockSpec((tm, V7X_LANES), row)
    return pl.pallas_call(
        _route_kernel,
        grid=(t // tm,),
        in_specs=[blk],
        out_specs=[blk, blk, blk, pl.BlockSpec((8, V7X_LANES), lambda i: (0, 0))],
        out_shape=[jax.ShapeDtypeStruct((t, V7X_LANES), I32), jax.ShapeDtypeStruct((t, V7X_LANES), F32),
                   jax.ShapeDtypeStruct((t, V7X_LANES), I32), jax.ShapeDtypeStruct((8, V7X_LANES), F32)],
        scratch_shapes=[pltpu.VMEM((8, V7X_LANES), F32)],
        compiler_params=_cparams(("arbitrary",)),
        name="route",
    )(logits)


def _row_copy(src, src_row, dst, dst_row, sem):
    return pltpu.make_async_copy(src.at[pl.ds(src_row, 1)], dst.at[pl.ds(dst_row, 1)], sem)


def _dispatch_kernel(dest_ref, hn_ref, xs_in_ref, xs_ref, sem):
    del xs_in_ref
    tm = hn_ref.shape[0]

    def issue(g, c):
        for u in range(DMA_ISSUE_UNROLL):
            r = g * DMA_ISSUE_UNROLL + u
            for k in range(TOP_K):
                _row_copy(hn_ref, r, xs_ref, dest_ref[r * TOP_K + k], sem).start(priority=(u * TOP_K + k) % 2)
        return c

    lax.fori_loop(0, tm // DMA_ISSUE_UNROLL, issue, 0)
    for k in range(TOP_K):
        pltpu.make_async_copy(hn_ref, xs_ref.at[pl.ds(0, tm)], sem).wait()


def _dispatch(dest_flat, hn, xs):
    t, dm = hn.shape
    tm = _row_tile(t, 128)
    return pl.pallas_call(
        _dispatch_kernel,
        grid=(t // tm,),
        in_specs=[pl.BlockSpec((tm * TOP_K,), lambda i: (i,), memory_space=pltpu.SMEM),
                  pl.BlockSpec((tm, dm), lambda i: (i, 0)),
                  pl.BlockSpec(memory_space=pl.ANY)],
        out_specs=pl.BlockSpec(memory_space=pl.ANY),
        out_shape=jax.ShapeDtypeStruct(xs.shape, xs.dtype),
        scratch_shapes=[pltpu.SemaphoreType.DMA(())],
        input_output_aliases={2: 0},
        compiler_params=_cparams(("arbitrary",)),
        name="moe_dispatch",
    )(dest_flat, hn, xs)


def _expert_kernel(be_ref, nu_ref, xs_ref, w1_ref, b1g_ref, b1l_ref, w2_ref, b2_ref, o_ref, w1g_s, w1l_s, w2_s):
    i = pl.program_id(0)
    e = be_ref[i]
    prev = be_ref[jnp.maximum(i - 1, 0)]

    @pl.when((i == 0) | (e != prev))
    def _():
        cw = 2 * V7X_LANES
        r = lax.broadcasted_iota(I32, (cw, cw), 0)
        c = lax.broadcasted_iota(I32, (cw, cw), 1)
        src_col = jnp.where(c < V7X_LANES, 2 * c, 2 * (c - V7X_LANES) + 1)
        sel = jnp.where(r == src_col, 1.0, 0.0).astype(BF16)
        for j in range(w1_ref.shape[2] // cw):
            t = jnp.dot(w1_ref[0, :, j * cw:(j + 1) * cw].astype(BF16), sel, preferred_element_type=F32)
            w1g_s[:, j * V7X_LANES:(j + 1) * V7X_LANES] = t[:, :V7X_LANES].astype(BF16)
            w1l_s[:, j * V7X_LANES:(j + 1) * V7X_LANES] = t[:, V7X_LANES:].astype(BF16)
        w2_s[...] = w2_ref[0].astype(BF16)

    @pl.when(i < nu_ref[0])
    def _():
        x = xs_ref[...].astype(BF16)
        ug = jnp.dot(x, w1g_s[...], preferred_element_type=F32) + b1g_ref[0]
        ul = jnp.dot(x, w1l_s[...], preferred_element_type=F32) + b1l_ref[0]
        x_glu = jnp.minimum(ug, SWIGLU_LIMIT)
        x_lin = jnp.clip(ul, -SWIGLU_LIMIT, SWIGLU_LIMIT)
        act = x_glu * jax.nn.sigmoid(SWIGLU_ALPHA * x_glu) * (x_lin + 1.0)
        o_ref[...] = jnp.dot(act.astype(BF16), w2_s[...], preferred_element_type=F32) + b2_ref[0]

    @pl.when(i >= nu_ref[0])
    def _():
        o_ref[...] = jnp.zeros(o_ref.shape, F32)


def _experts(block_exp, n_used, xs, w1, b1g, b1l, w2, b2):
    cap, dm = xs.shape
    dff = w2.shape[1]
    n_blocks = cap // MOE_BLOCK
    row_in = lambda i, be, nu: (jnp.minimum(i, nu[0] - 1), 0)
    row_out = lambda i, be, nu: (i, 0)
    wsel = lambda i, be, nu: (be[i], 0, 0)
    grid_spec = pltpu.PrefetchScalarGridSpec(
        num_scalar_prefetch=2,
        grid=(n_blocks,),
        in_specs=[pl.BlockSpec((MOE_BLOCK, dm), row_in),
                  pl.BlockSpec((1, dm, 2 * dff), wsel),
                  pl.BlockSpec((1, 1, dff), wsel), pl.BlockSpec((1, 1, dff), wsel),
                  pl.BlockSpec((1, dff, dm), wsel), pl.BlockSpec((1, 1, dm), wsel)],
        out_specs=pl.BlockSpec((MOE_BLOCK, dm), row_out),
        scratch_shapes=[pltpu.VMEM((dm, dff), BF16), pltpu.VMEM((dm, dff), BF16), pltpu.VMEM((dff, dm), BF16)],
    )
    return pl.pallas_call(
        _expert_kernel,
        grid_spec=grid_spec,
        out_shape=jax.ShapeDtypeStruct((cap, dm), F32),
        compiler_params=_cparams(("arbitrary",)),
        name="moe_experts",
    )(block_exp, n_used, xs, w1, b1g, b1l, w2, b2)


def _combine_kernel(dest_ref, gate_ref, h1_ref, gf_ref, eo_ref, y_ref, buf, sem):
    tm = h1_ref.shape[0]

    def issue(g, c):
        for u in range(DMA_ISSUE_UNROLL):
            r = g * DMA_ISSUE_UNROLL + u
            for k in range(TOP_K):
                _row_copy(eo_ref, dest_ref[r * TOP_K + k], buf.at[k], r, sem).start(priority=(u * TOP_K + k) % 2)
        return c

    lax.fori_loop(0, tm // DMA_ISSUE_UNROLL, issue, 0)
    for k in range(TOP_K):
        pltpu.make_async_copy(eo_ref.at[pl.ds(0, tm)], buf.at[k], sem).wait()
    gate = gate_ref[...]
    y = h1_ref[...]
    for k in range(TOP_K):
        y = y + gate[:, k:k + 1] * buf[k]
    y_ref[...] = _rms(y, gf_ref[...])


def _combine(dest_flat, gate, h1, gf, eo):
    t, dm = h1.shape
    tm = _row_tile(t, 128)
    row = lambda i: (i, 0)
    return pl.pallas_call(
        _combine_kernel,
        grid=(t // tm,),
        in_specs=[pl.BlockSpec((tm * TOP_K,), lambda i: (i,), memory_space=pltpu.SMEM),
                  pl.BlockSpec((tm, V7X_LANES), row), pl.BlockSpec((tm, dm), row),
                  pl.BlockSpec((1, dm), lambda i: (0, 0)), pl.BlockSpec(memory_space=pl.ANY)],
        out_specs=pl.BlockSpec((tm, dm), row),
        out_shape=jax.ShapeDtypeStruct((t, dm), F32),
        scratch_shapes=[pltpu.VMEM((TOP_K, tm, dm), F32), pltpu.SemaphoreType.DMA(())],
        compiler_params=_cparams(("arbitrary",)),
        name="moe_combine",
    )(dest_flat, gate, h1, gf, eo)


def _moe_and_final_norm(parts, router_b_unused, w1, b1, w2, b2, normf_g):
    del router_b_unused
    sizes = [p[0].shape[0] for p in parts]
    t = sum(sizes)
    dm = parts[0][0].shape[1]
    logits = jnp.concatenate([p[2] for p in parts], axis=0)
    idx, gate, rank, counts = _route(logits)
    cnt = counts[0, :N_EXPERTS].astype(I32)
    padded = ((cnt + MOE_BLOCK - 1) // MOE_BLOCK) * MOE_BLOCK
    pad_ends = jnp.cumsum(padded)
    pad_starts = pad_ends - padded
    n_blocks = -(-(t * TOP_K + N_EXPERTS * (MOE_BLOCK - 1)) // MOE_BLOCK)
    cap = n_blocks * MOE_BLOCK
    block_start = jnp.arange(n_blocks, dtype=I32) * MOE_BLOCK
    block_exp = jnp.minimum(jnp.sum(pad_ends[None, :] <= block_start[:, None], axis=1), N_EXPERTS - 1).astype(I32)
    n_used = (pad_ends[-1:] // MOE_BLOCK).astype(I32)
    dest = (pad_starts[idx[:, :TOP_K]] + rank[:, :TOP_K]).astype(I32).reshape(-1)

    xs = jnp.zeros((cap, dm), F32)
    off = 0
    for (h1, hn, _), n in zip(parts, sizes):
        xs = _dispatch(dest[off * TOP_K:(off + n) * TOP_K], hn, xs)
        off += n
    b1g = b1[:, None, 0::2]
    b1l = b1[:, None, 1::2]
    eo = _experts(block_exp, n_used, xs, w1, b1g, b1l, w2, b2[:, None, :])
    outs = []
    off = 0
    for (h1, hn, _), n in zip(parts, sizes):
        outs.append(_combine(dest[off * TOP_K:(off + n) * TOP_K], gate[off:off + n], h1, normf_g, eo))
        off += n
    return outs


def _rope_tables(pos, half):
    inv = ROPE_BASE ** (-jnp.arange(half, dtype=F32) / half)
    ang = pos.astype(F32)[:, None] * inv[None, :]
    return jnp.cos(ang), jnp.sin(ang)


def kernel(x_prompt, x_sample, cache_k, cache_v, state_ret, cache_mem_k, cache_mem_v, page_table, mem_prompt,
           norm1_g, w_in, lambda_q1, lambda_k1, lambda_q2, lambda_k2, subln_g, norm_mem_g, w_mem_kv, w_branch,
           w_out, norm2_g, router_w, router_b, w1, b1, w2, b2, normf_g):
    b, s, dm = x_prompt.shape
    db, t, _ = x_sample.shape
    depth = w_in.shape[0]
    assert depth == 1, "final norm is fused into the last layer's MoE combine; one layer supported"
    assert b == 1
    page = cache_k.shape[2]
    past_len = page_table.shape[1] * page
    dk_ret = dm // H_RET
    n_mem = mem_prompt.shape[1]

    cos_p, sin_p = _rope_tables(jnp.arange(s), dk_ret // 2)
    cos_s, sin_s = _rope_tables(past_len + jnp.arange(t), dk_ret // 2)
    cos_s = jnp.tile(cos_s, (db, 1))
    sin_s = jnp.tile(sin_s, (db, 1))

    l = 0
    lam_init = 0.8 - 0.6 * math.exp(-0.3 * l)
    lam4 = jnp.stack([lambda_q1[l], lambda_k1[l], lambda_q2[l], lambda_k2[l]]).astype(F32)
    g1 = norm1_g[l][None, :]
    sub_g = subln_g[l][None, :]
    d3 = 3 * dm
    w_l = w_in[l]
    w_diff = w_l[:, :d3].astype(BF16)
    w_ret = w_l[:, d3:d3 + 4 * dm].astype(BF16)
    w_mg = w_l[:, d3 + 4 * dm:].astype(BF16)
    wb = w_branch[l].astype(BF16)
    wo = w_out[l].astype(BF16)
    g2 = norm2_g[l][None, :]
    rw = jnp.zeros((dm, V7X_LANES), F32).at[:, :N_EXPERTS].set(router_w[l]).astype(BF16)
    rb = jnp.full((1, V7X_LANES), NEG_INF, F32).at[0, :N_EXPERTS].set(router_b[l])

    xp = x_prompt.reshape(s, dm)
    q_d, k_f, v_f, k_b, v_t = _proj_diff(xp, g1, w_diff, BF16)
    y_d = _diff_attn_prompt(q_d, k_b, v_t, lam4, subln_g[l][:, None], lam_init)
    rq, rk, rv, rg = _proj_ret(xp, g1, w_ret, cos_p, sin_p)
    to3 = lambda a: a.reshape(1, s, dm)
    y_r, st_p = _retention(to3(rq), to3(rk), to3(rv), to3(rg), jnp.zeros((1, H_RET, dk_ret, dk_ret), F32),
                           chunk=_row_tile(s, RET_CHUNK), y_dtype=BF16, mm_dtype=BF16)
    mq, gt = _proj_mem_gate(xp, g1, w_mg, BF16)
    mkv = _mem_kv(mem_prompt.reshape(n_mem, dm), norm_mem_g[l][None, :], w_mem_kv[l].astype(BF16))
    mk, mv = mkv[:, :dm], mkv[:, dm:]
    y_m = _mem_attn(mq[None], mk[None], mv[None], BF16)
    part_p = _merge(xp, y_d, y_r.reshape(s, dm), y_m.reshape(s, dm), gt, wb, wo, g2, rw, rb)

    ns = db * t
    xs_ = x_sample.reshape(ns, dm)
    q_s, ks_f, vs_f, _, _ = _proj_diff(xs_, g1, w_diff, F32)
    k_s5 = ks_f.reshape(db, t, H_DIFF, D_V)
    v_s5 = vs_f.reshape(db, t, H_DIFF, D_V)
    y_ds = _diff_attn_sample(page_table, q_s.reshape(db, t, dm), k_s5, v_s5, cache_k, cache_v, l,
                             lam4, sub_g, lam_init)
    rq, rk, rv, rg = _proj_ret(xs_, g1, w_ret, cos_s, sin_s)
    tos = lambda a: a.reshape(db, t, dm)
    y_rs, st_s = _retention(tos(rq), tos(rk), tos(rv), tos(rg), state_ret[l].astype(F32),
                            chunk=t, y_dtype=F32, mm_dtype=F32)
    mq_s, gt_s = _proj_mem_gate(xs_, g1, w_mg, F32)
    y_ms = _mem_attn(mq_s.reshape(db, t, dm), cache_mem_k[l].reshape(db, n_mem, dm),
                     cache_mem_v[l].reshape(db, n_mem, dm), F32)
    part_s = _merge(xs_, y_ds.reshape(ns, dm), y_rs.reshape(ns, dm), y_ms.reshape(ns, dm), gt_s, wb, wo, g2, rw, rb)

    y_p, y_s = _moe_and_final_norm([part_p, part_s], None, w1[l], b1[l], w2[l], b2[l], normf_g[None, :])

    return (y_p.reshape(b, s, dm), y_s.reshape(db, t, dm),
            k_f.reshape(1, b, s, H_DIFF, D_V), v_f.reshape(1, b, s, H_DIFF, D_V),
            st_p.reshape(1, b, H_RET, dk_ret, dk_ret),
            mk.reshape(1, b, n_mem, H_MEM, dm // H_MEM), mv.reshape(1, b, n_mem, H_MEM, dm // H_MEM),
            k_s5[None], v_s5[None], st_s[None].astype(state_ret.dtype))
```

```python
import functools
import math

import jax
import jax.numpy as jnp
import numpy as np
from jax import lax
from jax.experimental import pallas as pl
from jax.experimental.pallas import tpu as pltpu

F32 = jnp.float32
BF16 = jnp.bfloat16
I32 = jnp.int32

H_DIFF = 8
D_QK = 64
D_V = 128
H_RET = 4
H_MEM = 4
N_BRANCH = 3
N_EXPERTS = 32
TOP_K = 4
SWIGLU_LIMIT = 7.0
SWIGLU_ALPHA = 1.702
ROPE_BASE = 10000.0
RMS_EPS = 1e-6

V7X_LANES = 128
V7X_VMEM_LIMIT_BYTES = 56 * 1024 * 1024

MOE_BLOCK = 256
DMA_ISSUE_UNROLL = 16
SAMPLE_PAGES_PER_STEP = 16
SAMPLE_PAGES_PER_UPDATE = 4
RET_CHUNK = 256
NEG_INF = float("-inf")
DIFF_Q_SCALE = (D_QK ** -0.5) * math.log2(math.e)


def _cparams(sem):
    return pltpu.CompilerParams(dimension_semantics=sem, vmem_limit_bytes=V7X_VMEM_LIMIT_BYTES)


def _rms(x, g=None):
    y = x * lax.rsqrt(jnp.mean(x * x, axis=-1, keepdims=True) + RMS_EPS)
    return y if g is None else y * g


def _row_tiles(dm):
    return (dm // V7X_LANES, V7X_LANES)


def _row_tile(n, pref):
    t = min(n, pref)
    while n % t:
        t //= 2
    return t


def _proj_diff_kernel(x_ref, g_ref, w_ref, q_ref, kf_ref, vf_ref, kb_ref, vt_ref):
    xn = _rms(x_ref[...], g_ref[...]).astype(BF16)
    d = kf_ref.shape[-1]
    q = jnp.dot(xn, w_ref[:, 0:d], preferred_element_type=F32)
    q_ref[...] = (q * DIFF_Q_SCALE).astype(q_ref.dtype)
    k = jnp.dot(xn, w_ref[:, d:2 * d], preferred_element_type=F32)
    kf_ref[...] = k
    kb_ref[...] = k.astype(BF16)
    v = jnp.dot(xn, w_ref[:, 2 * d:3 * d], preferred_element_type=F32)
    vf_ref[...] = v
    vt_ref[...] = v.T.astype(BF16)


def _proj_diff(x, g, w, q_dtype):
    t, dm = x.shape
    tm = _row_tile(t, 512)
    row = lambda i: (i, 0)
    blk = pl.BlockSpec((tm, dm), row)
    return pl.pallas_call(
        _proj_diff_kernel,
        grid=(t // tm,),
        in_specs=[blk, pl.BlockSpec((1, dm), lambda i: (0, 0)), pl.BlockSpec(w.shape, lambda i: (0, 0))],
        out_specs=[blk] * 4 + [pl.BlockSpec((dm, tm), lambda i: (0, i))],
        out_shape=[jax.ShapeDtypeStruct((t, dm), q_dtype), jax.ShapeDtypeStruct((t, dm), F32),
                   jax.ShapeDtypeStruct((t, dm), F32), jax.ShapeDtypeStruct((t, dm), BF16),
                   jax.ShapeDtypeStruct((dm, t), BF16)],
        compiler_params=_cparams(("parallel",)),
        name="proj_diff",
    )(x, g, w)


def _proj_ret_kernel(x_ref, g_ref, w_ref, cos_ref, sin_ref, q_ref, k_ref, v_ref, gate_ref):
    xn = _rms(x_ref[...], g_ref[...]).astype(BF16)
    d = q_ref.shape[-1]
    dk = d // H_RET
    half = dk // 2
    cos = cos_ref[...]
    sin = sin_ref[...]

    def rope(u, scale):
        outs = []
        for h in range(H_RET):
            x1 = u[:, h * dk:h * dk + half]
            x2 = u[:, h * dk + half:(h + 1) * dk]
            outs.append((x1 * cos - x2 * sin) * scale)
            outs.append((x1 * sin + x2 * cos) * scale)
        return jnp.concatenate(outs, axis=-1)

    q_ref[...] = rope(jnp.dot(xn, w_ref[:, 0:d], preferred_element_type=F32), 1.0)
    k_ref[...] = rope(jnp.dot(xn, w_ref[:, d:2 * d], preferred_element_type=F32), dk ** -0.5)
    v_ref[...] = jnp.dot(xn, w_ref[:, 2 * d:3 * d], preferred_element_type=F32)
    gate_ref[...] = jnp.dot(xn, w_ref[:, 3 * d:4 * d], preferred_element_type=F32)


def _proj_ret(x, g, w, cos, sin):
    t, dm = x.shape
    tm = _row_tile(t, 512)
    row = lambda i: (i, 0)
    blk = pl.BlockSpec((tm, dm), row)
    tab = pl.BlockSpec((tm, cos.shape[1]), row)
    return pl.pallas_call(
        _proj_ret_kernel,
        grid=(t // tm,),
        in_specs=[blk, pl.BlockSpec((1, dm), lambda i: (0, 0)), pl.BlockSpec(w.shape, lambda i: (0, 0)), tab, tab],
        out_specs=[blk] * 4,
        out_shape=[jax.ShapeDtypeStruct((t, dm), F32)] * 4,
        compiler_params=_cparams(("parallel",)),
        name="proj_ret",
    )(x, g, w, cos, sin)


def _proj_mem_gate_kernel(x_ref, g_ref, w_ref, mq_ref, gt_ref):
    xn = _rms(x_ref[...], g_ref[...]).astype(BF16)
    d = mq_ref.shape[-1]
    mq = jnp.dot(xn, w_ref[:, 0:d], preferred_element_type=F32)
    mq_ref[...] = (mq * ((d // H_MEM) ** -0.5)).astype(mq_ref.dtype)
    gt_ref[...] = jnp.dot(xn, w_ref[:, d:], preferred_element_type=F32)


def _proj_mem_gate(x, g, w, mq_dtype):
    t, dm = x.shape
    tm = _row_tile(t, 512)
    row = lambda i: (i, 0)
    return pl.pallas_call(
        _proj_mem_gate_kernel,
        grid=(t // tm,),
        in_specs=[pl.BlockSpec((tm, dm), row), pl.BlockSpec((1, dm), lambda i: (0, 0)),
                  pl.BlockSpec(w.shape, lambda i: (0, 0))],
        out_specs=[pl.BlockSpec((tm, dm), row), pl.BlockSpec((tm, N_BRANCH * dm), row)],
        out_shape=[jax.ShapeDtypeStruct((t, dm), mq_dtype), jax.ShapeDtypeStruct((t, N_BRANCH * dm), F32)],
        compiler_params=_cparams(("parallel",)),
        name="proj_mem_gate",
    )(x, g, w)


def _lambda_value(lam_ref, lam_init):
    t = lam_ref[...]
    a = jnp.sum(t[0:1] * t[1:2], axis=-1, keepdims=True)
    b = jnp.sum(t[2:3] * t[3:4], axis=-1, keepdims=True)
    return jnp.exp(a) - jnp.exp(b) + lam_init


def _diff_attn_prompt_kernel(q_ref, k_ref, vt_ref, lam_ref, g_ref, o_ref, m_s, l_s, acc_s, sa_s, sb_s, mba_s, mbb_s,
                             *, tq, tk, rc, lam_init):
    qi = pl.program_id(1)
    q = q_ref[...]
    lane = lax.broadcasted_iota(I32, q.shape, 1)
    zero = jnp.zeros_like(q)
    qz = jnp.concatenate([jnp.where(lane < D_QK, q, zero), jnp.where(lane >= D_QK, q, zero)], axis=0)
    m_s[...] = jnp.full(m_s.shape, NEG_INF, F32)
    l_s[...] = jnp.zeros(l_s.shape, F32)
    acc_s[...] = jnp.zeros(acc_s.shape, F32)
    n_full = (qi * tq) // tk

    def scores(j, masked, s_buf, mb_buf):
        ks = k_ref[pl.ds(pl.multiple_of(j * tk, tk), tk), :]
        s = lax.dot_general(ks, qz, (((1,), (1,)), ((), ())), preferred_element_type=F32)
        if masked:
            kpos = lax.broadcasted_iota(I32, s.shape, 0) + j * tk
            col = lax.broadcasted_iota(I32, s.shape, 1)
            qpos = jnp.where(col >= tq, col - tq, col) + qi * tq
            s = jnp.where(kpos <= qpos, s, NEG_INF)
        s_buf[...] = s
        mb_buf[...] = jnp.max(s, axis=0, keepdims=True)

    def absorb(s_buf, mb_buf, vblock):
        m_old = m_s[...]
        m_new = jnp.maximum(m_old, mb_buf[...])
        alpha = jnp.exp2(m_old - m_new)
        part = jnp.zeros((8, 2 * tq), F32)
        vstart = pl.multiple_of(vblock * tk, tk)
        pv = None
        for r in range(tk // rc):
            p = jnp.exp2(s_buf[r * rc:(r + 1) * rc, :] - m_new)
            part = part + jnp.sum(p.reshape(rc // 8, 8, 2 * tq), axis=0)
            vt = vt_ref[:, pl.ds(vstart + r * rc, rc)]
            d = jnp.dot(vt, p.astype(BF16), preferred_element_type=F32)
            pv = d if pv is None else pv + d
        l_s[...] = alpha * l_s[...] + jnp.sum(part, axis=0, keepdims=True)
        acc_s[...] = alpha * acc_s[...] + pv
        m_s[...] = m_new

    scores(n_full, True, sa_s, mba_s)
    n_pairs = n_full // 2

    def body(i, c):
        scores(2 * i, False, sb_s, mbb_s)
        absorb(sa_s, mba_s, jnp.where(i == 0, n_full, 2 * i - 1))
        scores(2 * i + 1, False, sa_s, mba_s)
        absorb(sb_s, mbb_s, 2 * i)
        return c

    lax.fori_loop(0, n_pairs, body, 0)
    pending = jnp.where(n_pairs == 0, n_full, 2 * n_pairs - 1)

    @pl.when(n_full % 2 == 1)
    def _():
        scores(n_full - 1, False, sb_s, mbb_s)
        absorb(sa_s, mba_s, pending)
        absorb(sb_s, mbb_s, n_full - 1)

    @pl.when(n_full % 2 == 0)
    def _():
        absorb(sa_s, mba_s, pending)

    o = acc_s[...] / l_s[...]
    lam = _lambda_value(lam_ref, lam_init)
    od = o[:, :tq] - lam * o[:, tq:]
    y = od * lax.rsqrt(jnp.mean(od * od, axis=0, keepdims=True) + RMS_EPS) * g_ref[...] * (1.0 - lam_init)
    o_ref[...] = y.T.astype(o_ref.dtype)


def _diff_attn_prompt(q, k, vt, lam4, subln_g_col, lam_init):
    s, dm = q.shape
    tq = _row_tile(s, 512)
    tk = _row_tile(s, 1024)
    rc = _row_tile(tk, 256)
    kern = functools.partial(_diff_attn_prompt_kernel, tq=tq, tk=tk, rc=rc, lam_init=lam_init)
    row_stat = pltpu.VMEM((1, 2 * tq), F32)
    score_buf = pltpu.VMEM((tk, 2 * tq), F32)
    return pl.pallas_call(
        kern,
        grid=(H_DIFF, s // tq),
        in_specs=[pl.BlockSpec((tq, D_V), lambda h, i: (i, h)),
                  pl.BlockSpec((s, D_V), lambda h, i: (0, h)),
                  pl.BlockSpec((D_V, s), lambda h, i: (h, 0)),
                  pl.BlockSpec(lam4.shape, lambda h, i: (0, 0)),
                  pl.BlockSpec((D_V, 1), lambda h, i: (0, 0))],
        out_specs=pl.BlockSpec((tq, D_V), lambda h, i: (i, h)),
        out_shape=jax.ShapeDtypeStruct((s, dm), BF16),
        scratch_shapes=[row_stat, row_stat, pltpu.VMEM((D_V, 2 * tq), F32),
                        score_buf, score_buf, row_stat, row_stat],
        compiler_params=_cparams(("parallel", "parallel")),
        name="diff_attn_prompt",
    )(q, k, vt, lam4, subln_g_col)


def _diff_attn_sample_kernel(pt_ref, q_ref, kn_ref, vn_ref, lam_ref, g_ref, *rest, pps, lam_init):
    k_refs = rest[:pps]
    v_refs = rest[pps:2 * pps]
    o_ref, qmat_s, bias_s, m_s, l_s, acc_s, s_s = rest[2 * pps:]
    j = pl.program_id(1)
    nq = q_ref.shape[1]
    pc = bias_s.shape[1]

    @pl.when(j == 0)
    def _():
        qb = q_ref[0]
        lane = lax.broadcasted_iota(I32, (nq, D_V), 1)
        parts = []
        for h in range(H_DIFF):
            blk = qb[:, h * D_V:(h + 1) * D_V]
            parts.append(jnp.where(lane < D_QK, blk, 0.0))
            parts.append(jnp.where(lane >= D_QK, blk, 0.0))
        qmat_s[...] = jnp.concatenate(parts, axis=0).astype(BF16)
        r = lax.broadcasted_iota(I32, bias_s.shape, 0)
        c = lax.broadcasted_iota(I32, bias_s.shape, 1)
        bias_s[...] = jnp.where((c % H_DIFF) == (r // (2 * nq)), 0.0, NEG_INF)
        m_s[...] = jnp.full(m_s.shape, NEG_INF, F32)
        l_s[...] = jnp.zeros(l_s.shape, F32)
        acc_s[...] = jnp.zeros(acc_s.shape, F32)

    def flat(page):
        return page.reshape(page.shape[0] * page.shape[1], page.shape[2]).astype(BF16)

    qmat = qmat_s[...]
    grp = s_s.shape[1] // pc
    for g0 in range(0, pps, grp):
        mb = None
        for r in range(grp):
            s = lax.dot_general(qmat, flat(k_refs[g0 + r][0, 0]), (((1,), (1,)), ((), ())),
                                preferred_element_type=F32) + bias_s[...]
            s_s[:, r * pc:(r + 1) * pc] = s
            mr = jnp.max(s, axis=-1, keepdims=True)
            mb = mr if mb is None else jnp.maximum(mb, mr)
        m_old = m_s[...]
        m_new = jnp.maximum(m_old, mb)
        alpha = jnp.exp2(m_old - m_new)
        lsum = None
        pv = None
        for r in range(grp):
            p = jnp.exp2(s_s[:, r * pc:(r + 1) * pc] - m_new)
            ls = jnp.sum(p, axis=-1, keepdims=True)
            lsum = ls if lsum is None else lsum + ls
            d = jnp.dot(p.astype(BF16), flat(v_refs[g0 + r][0, 0]), preferred_element_type=F32)
            pv = d if pv is None else pv + d
        l_s[...] = alpha * l_s[...] + lsum
        acc_s[...] = alpha * acc_s[...] + pv
        m_s[...] = m_new

    @pl.when(j == pl.num_programs(1) - 1)
    def _():
        s = lax.dot_general(qmat, flat(kn_ref[0]), (((1,), (1,)), ((), ())), preferred_element_type=F32)
        r = lax.broadcasted_iota(I32, s.shape, 0)
        c = lax.broadcasted_iota(I32, s.shape, 1)
        keep = ((c % H_DIFF) == (r // (2 * nq))) & ((c // H_DIFF) <= (r % nq))
        s = jnp.where(keep, s, NEG_INF)
        m_o = m_s[...]
        m_n = jnp.maximum(m_o, jnp.max(s, axis=-1, keepdims=True))
        al = jnp.exp2(m_o - m_n)
        p = jnp.exp2(s - m_n)
        l = al * l_s[...] + jnp.sum(p, axis=-1, keepdims=True)
        acc = al * acc_s[...] + jnp.dot(p.astype(BF16), flat(vn_ref[0]), preferred_element_type=F32)
        o = acc / l
        lam = _lambda_value(lam_ref, lam_init)
        g = g_ref[...]
        outs = []
        for h in range(H_DIFF):
            base = h * 2 * nq
            od = o[base:base + nq] - lam * o[base + nq:base + 2 * nq]
            outs.append(_rms(od, g) * (1.0 - lam_init))
        o_ref[0] = jnp.concatenate(outs, axis=-1)


def _diff_attn_sample(page_table, q, k_new, v_new, cache_k, cache_v, layer, lam4, subln_g, lam_init,
                      pps=SAMPLE_PAGES_PER_STEP, grp=SAMPLE_PAGES_PER_UPDATE):
    b, nq, dm = q.shape
    n_pages = page_table.shape[1]
    while n_pages % pps:
        pps //= 2
    grp = min(grp, pps)
    page_shape = (1, 1) + cache_k.shape[2:]
    page_specs = [pl.BlockSpec(page_shape, functools.partial(
        lambda bi, j, pt, r: (layer, pt[bi, j * pps + r], 0, 0, 0), r=r)) for r in range(pps)]
    rows = H_DIFF * 2 * nq
    cols = cache_k.shape[2] * cache_k.shape[3]
    kern = functools.partial(_diff_attn_sample_kernel, pps=pps, lam_init=lam_init)
    per_seq3 = lambda bi, j, pt: (bi, 0, 0)
    per_seq4 = lambda bi, j, pt: (bi, 0, 0, 0)
    const2 = lambda bi, j, pt: (0, 0)
    grid_spec = pltpu.PrefetchScalarGridSpec(
        num_scalar_prefetch=1,
        grid=(b, n_pages // pps),
        in_specs=[pl.BlockSpec((1, nq, dm), per_seq3),
                  pl.BlockSpec((1,) + k_new.shape[1:], per_seq4),
                  pl.BlockSpec((1,) + v_new.shape[1:], per_seq4),
                  pl.BlockSpec(lam4.shape, const2),
                  pl.BlockSpec((1, D_V), const2)] + page_specs + page_specs,
        out_specs=pl.BlockSpec((1, nq, dm), per_seq3),
        scratch_shapes=[pltpu.VMEM((rows, D_V), BF16), pltpu.VMEM((rows, cols), F32),
                        pltpu.VMEM((rows, 1), F32), pltpu.VMEM((rows, 1), F32), pltpu.VMEM((rows, D_V), F32),
                        pltpu.VMEM((rows, grp * cols), F32)],
    )
    return pl.pallas_call(
        kern,
        grid_spec=grid_spec,
        out_shape=jax.ShapeDtypeStruct((b, nq, dm), F32),
        compiler_params=_cparams(("parallel", "arbitrary")),
        name="diff_attn_sample",
    )(page_table, q, k_new, v_new, lam4, subln_g, *([cache_k] * pps), *([cache_v] * pps))


def _retention_log_decay(h):
    return float(np.log(np.float32(1.0) - np.exp2(np.float32(-5.0 - h))))


def _retention_kernel(q_ref, k_ref, v_ref, gate_ref, s0_ref, y_ref, sout_ref, state_s, *, mm_dtype):
    c = pl.program_id(1)

    @pl.when(c == 0)
    def _():
        state_s[...] = s0_ref[0]

    cl = q_ref.shape[1]
    dk = state_s.shape[1]
    ii = lax.broadcasted_iota(I32, (cl, cl), 0)
    jj = lax.broadcasted_iota(I32, (cl, cl), 1)
    dist = (ii - jj).astype(F32)
    ri = lax.broadcasted_iota(I32, (cl, 1), 0).astype(F32)
    for h in range(H_RET):
        log_g = _retention_log_decay(h)
        sl = slice(h * dk, (h + 1) * dk)
        q = q_ref[0, :, sl]
        k = k_ref[0, :, sl]
        decay = jnp.where(dist >= 0, jnp.exp(jnp.maximum(dist, 0.0) * log_g), 0.0)
        row_decay = jnp.exp((ri + 1.0) * log_g)
        col_decay = jnp.exp((cl - 1.0 - ri) * log_g)
        qm = q.astype(mm_dtype)
        vm = v_ref[0, :, sl].astype(mm_dtype)
        state = state_s[h]
        sc = lax.dot_general(qm, k.astype(mm_dtype), (((1,), (1,)), ((), ())), preferred_element_type=F32) * decay
        o = jnp.dot(sc.astype(mm_dtype), vm, preferred_element_type=F32)
        o = o + jnp.dot(qm, state.astype(mm_dtype), preferred_element_type=F32) * row_decay
        kw = (k * col_decay).astype(mm_dtype)
        upd = lax.dot_general(kw, vm, (((0,), (0,)), ((), ())), preferred_element_type=F32)
        state_s[h] = math.exp(cl * log_g) * state + upd
        g = gate_ref[0, :, sl]
        y_ref[0, :, sl] = (g * jax.nn.sigmoid(g) * _rms(o)).astype(y_ref.dtype)

    @pl.when(c == pl.num_programs(1) - 1)
    def _():
        sout_ref[0] = state_s[...]


def _retention(q, k, v, gate, state0, chunk, y_dtype, mm_dtype):
    b, s, dm = q.shape
    nc = s // chunk
    blk = pl.BlockSpec((1, chunk, dm), lambda bi, c: (bi, c, 0))
    st = pl.BlockSpec((1,) + state0.shape[1:], lambda bi, c: (bi, 0, 0, 0))
    return pl.pallas_call(
        functools.partial(_retention_kernel, mm_dtype=mm_dtype),
        grid=(b, nc),
        in_specs=[blk, blk, blk, blk, st],
        out_specs=[blk, st],
        out_shape=[jax.ShapeDtypeStruct((b, s, dm), y_dtype), jax.ShapeDtypeStruct(state0.shape, F32)],
        scratch_shapes=[pltpu.VMEM(state0.shape[1:], F32)],
        compiler_params=_cparams(("parallel", "arbitrary")),
        name="retention",
    )(q, k, v, gate, state0)


def _mem_kv_kernel(mem_ref, g_ref, w_ref, o_ref):
    xn = _rms(mem_ref[...], g_ref[...]).astype(BF16)
    o_ref[...] = jnp.dot(xn, w_ref[...], preferred_element_type=F32)


def _mem_kv(mem, g, w):
    m, dm = mem.shape
    n = w.shape[1]
    return pl.pallas_call(
        _mem_kv_kernel,
        grid=(n // dm,),
        in_specs=[pl.BlockSpec((m, dm), lambda i: (0, 0)), pl.BlockSpec((1, dm), lambda i: (0, 0)),
                  pl.BlockSpec((dm, dm), lambda i: (0, i))],
        out_specs=pl.BlockSpec((m, dm), lambda i: (0, i)),
        out_shape=jax.ShapeDtypeStruct((m, n), F32),
        compiler_params=_cparams(("parallel",)),
        name="mem_kv",
    )(mem, g, w)


def _mem_attn_kernel(q_ref, mk_ref, mv_ref, o_ref):
    q = q_ref[0].astype(BF16)
    mk = mk_ref[0].astype(BF16)
    mv = mv_ref[0].astype(BF16)
    d = q.shape[-1] // H_MEM
    outs = []
    for h in range(H_MEM):
        sl = slice(h * d, (h + 1) * d)
        s = lax.dot_general(q[:, sl], mk[:, sl], (((1,), (1,)), ((), ())), preferred_element_type=F32)
        m = jnp.max(s, axis=-1, keepdims=True)
        p = jnp.exp(s - m)
        p = p / jnp.sum(p, axis=-1, keepdims=True)
        outs.append(jnp.dot(p.astype(BF16), mv[:, sl], preferred_element_type=F32))
    o_ref[0] = jnp.concatenate(outs, axis=-1).astype(o_ref.dtype)


def _mem_attn(q, mk, mv, y_dtype):
    b, t, dm = q.shape
    tm = _row_tile(t, 512)
    qb = pl.BlockSpec((1, tm, dm), lambda bi, i: (bi, i, 0))
    mb = pl.BlockSpec((1,) + mk.shape[1:], lambda bi, i: (bi, 0, 0))
    return pl.pallas_call(
        _mem_attn_kernel,
        grid=(b, t // tm),
        in_specs=[qb, mb, mb],
        out_specs=qb,
        out_shape=jax.ShapeDtypeStruct((b, t, dm), y_dtype),
        compiler_params=_cparams(("parallel", "parallel")),
        name="mem_attn",
    )(q, mk, mv)


def _merge_kernel(x_ref, yd_ref, yr_ref, ym_ref, gt_ref, wb_ref, wo_ref, g2_ref, rw_ref, rb_ref,
                  h1_ref, hn_ref, lg_ref):
    dm = x_ref.shape[-1]
    mixed = jnp.zeros(x_ref.shape, F32)
    for i, y_ref in enumerate((yd_ref, yr_ref, ym_ref)):
        proj = jnp.dot(y_ref[...].astype(BF16), wb_ref[i], preferred_element_type=F32)
        mixed = mixed + jax.nn.sigmoid(gt_ref[:, i * dm:(i + 1) * dm]) * proj
    h1 = x_ref[...] + jnp.dot(mixed.astype(BF16), wo_ref[...], preferred_element_type=F32)
    h1_ref[...] = h1
    hn = _rms(h1, g2_ref[...])
    hn_ref[...] = hn.reshape(hn_ref.shape)
    lg_ref[...] = jnp.dot(hn.astype(BF16), rw_ref[...], preferred_element_type=F32) + rb_ref[...]


def _merge(x, yd, yr, ym, gt, wb, wo, g2, rw, rb):
    t, dm = x.shape
    tm = _row_tile(t, 512)
    row = lambda i: (i, 0)
    blk = pl.BlockSpec((tm, dm), row)
    c2 = lambda i: (0, 0)
    return pl.pallas_call(
        _merge_kernel,
        grid=(t // tm,),
        in_specs=[blk, blk, blk, blk, pl.BlockSpec((tm, N_BRANCH * dm), row),
                  pl.BlockSpec(wb.shape, lambda i: (0, 0, 0)), pl.BlockSpec(wo.shape, c2),
                  pl.BlockSpec((1, dm), c2), pl.BlockSpec(rw.shape, c2), pl.BlockSpec(rb.shape, c2)],
        out_specs=[blk, pl.BlockSpec((tm,) + _row_tiles(dm), lambda i: (i, 0, 0)), pl.BlockSpec((tm, V7X_LANES), row)],
        out_shape=[jax.ShapeDtypeStruct((t, dm), F32), jax.ShapeDtypeStruct((t,) + _row_tiles(dm), F32),
                   jax.ShapeDtypeStruct((t, V7X_LANES), F32)],
        compiler_params=_cparams(("parallel",)),
        name="merge",
    )(x, yd, yr, ym, gt, wb, wo, g2, rw, rb)


def _route_kernel(lg_ref, idx_ref, gate_ref, rank_ref, cnt_ref, carry_s):
    i = pl.program_id(0)

    @pl.when(i == 0)
    def _():
        carry_s[...] = jnp.zeros(carry_s.shape, F32)

    l = lg_ref[...]
    tm = l.shape[0]
    lane = lax.broadcasted_iota(I32, l.shape, 1)
    vals, idxs, hots = [], [], []
    for _ in range(TOP_K):
        m = jnp.max(l, axis=-1, keepdims=True)
        ik = jnp.min(jnp.where(l == m, lane, V7X_LANES), axis=-1, keepdims=True)
        hot = lane == ik
        vals.append(m)
        idxs.append(ik)
        hots.append(hot)
        l = jnp.where(hot, NEG_INF, l)
    es = [jnp.exp(v - vals[0]) for v in vals]
    den = es[0] + es[1] + es[2] + es[3]
    picked = jnp.zeros(l.shape, F32)
    for hot in hots:
        picked = picked + jnp.where(hot, 1.0, 0.0)
    r = lax.broadcasted_iota(I32, (tm, tm), 0)
    c = lax.broadcasted_iota(I32, (tm, tm), 1)
    before = jnp.where(c < r, 1.0, 0.0).astype(BF16)
    cum = jnp.dot(before, picked.astype(BF16), preferred_element_type=F32) + carry_s[0:1, :]
    idx_o = jnp.zeros(l.shape, I32)
    gate_o = jnp.zeros(l.shape, F32)
    rank_o = jnp.zeros(l.shape, I32)
    for k in range(TOP_K):
        rk = jnp.sum(jnp.where(hots[k], cum, 0.0), axis=-1, keepdims=True).astype(I32)
        idx_o = jnp.where(lane == k, idxs[k], idx_o)
        gate_o = jnp.where(lane == k, es[k] / den, gate_o)
        rank_o = jnp.where(lane == k, rk, rank_o)
    idx_ref[...] = idx_o
    gate_ref[...] = gate_o
    rank_ref[...] = rank_o
    carry_s[...] = carry_s[...] + jnp.sum(picked, axis=0, keepdims=True)
    cnt_ref[...] = carry_s[...]


def _route(logits):
    t = logits.shape[0]
    tm = _row_tile(t, 512)
    row = lambda i: (i, 0)
    blk = pl.BlockSpec((tm, V7X_LANES), row)
    return pl.pallas_call(
        _route_kernel,
        grid=(t // tm,),
        in_specs=[blk],
        out_specs=[blk, blk, blk, pl.BlockSpec((8, V7X_LANES), lambda i: (0, 0))],
        out_shape=[jax.ShapeDtypeStruct((t, V7X_LANES), I32), jax.ShapeDtypeStruct((t, V7X_LANES), F32),
                   jax.ShapeDtypeStruct((t, V7X_LANES), I32), jax.ShapeDtypeStruct((8, V7X_LANES), F32)],
        scratch_shapes=[pltpu.VMEM((8, V7X_LANES), F32)],
        compiler_params=_cparams(("arbitrary",)),
        name="route",
    )(logits)


def _row_copy(src, src_row, dst, dst_row, sem):
    return pltpu.make_async_copy(src.at[pl.ds(src_row, 1)], dst.at[pl.ds(dst_row, 1)], sem)


def _dispatch_kernel(dest_ref, hn_ref, xs_in_ref, xs_ref, sem):
    del xs_in_ref
    tm = hn_ref.shape[0]

    def issue(g, c):
        for u in range(DMA_ISSUE_UNROLL):
            r = g * DMA_ISSUE_UNROLL + u
            for k in range(TOP_K):
                _row_copy(hn_ref, r, xs_ref, dest_ref[r * TOP_K + k], sem).start(priority=(u * TOP_K + k) % 2)
        return c

    lax.fori_loop(0, tm // DMA_ISSUE_UNROLL, issue, 0)
    for k in range(TOP_K):
        pltpu.make_async_copy(hn_ref, xs_ref.at[pl.ds(0, tm)], sem).wait()


def _dispatch(dest_flat, hn, xs):
    t = hn.shape[0]
    tm = _row_tile(t, 128)
    return pl.pallas_call(
        _dispatch_kernel,
        grid=(t // tm,),
        in_specs=[pl.BlockSpec((tm * TOP_K,), lambda i: (i,), memory_space=pltpu.SMEM),
                  pl.BlockSpec((tm,) + hn.shape[1:], lambda i: (i, 0, 0)),
                  pl.BlockSpec(memory_space=pl.ANY)],
        out_specs=pl.BlockSpec(memory_space=pl.ANY),
        out_shape=jax.ShapeDtypeStruct(xs.shape, xs.dtype),
        scratch_shapes=[pltpu.SemaphoreType.DMA(())],
        input_output_aliases={2: 0},
        compiler_params=_cparams(("arbitrary",)),
        name="moe_dispatch",
    )(dest_flat, hn, xs)


def _expert_kernel(be_ref, nu_ref, xs_ref, w1_ref, b1g_ref, b1l_ref, w2_ref, b2_ref, o_ref, w1g_s, w1l_s, w2_s):
    i = pl.program_id(0)
    e = be_ref[i]
    prev = be_ref[jnp.maximum(i - 1, 0)]

    @pl.when((i == 0) | (e != prev))
    def _():
        cw = 2 * V7X_LANES
        r = lax.broadcasted_iota(I32, (cw, cw), 0)
        c = lax.broadcasted_iota(I32, (cw, cw), 1)
        src_col = jnp.where(c < V7X_LANES, 2 * c, 2 * (c - V7X_LANES) + 1)
        sel = jnp.where(r == src_col, 1.0, 0.0).astype(BF16)
        for j in range(w1_ref.shape[2] // cw):
            t = jnp.dot(w1_ref[0, :, j * cw:(j + 1) * cw].astype(BF16), sel, preferred_element_type=F32)
            w1g_s[:, j * V7X_LANES:(j + 1) * V7X_LANES] = t[:, :V7X_LANES].astype(BF16)
            w1l_s[:, j * V7X_LANES:(j + 1) * V7X_LANES] = t[:, V7X_LANES:].astype(BF16)
        w2_s[...] = w2_ref[0].astype(BF16)

    @pl.when(i < nu_ref[0])
    def _():
        x = xs_ref[...].reshape(xs_ref.shape[0], w2_s.shape[1]).astype(BF16)
        ug = jnp.dot(x, w1g_s[...], preferred_element_type=F32) + b1g_ref[0]
        ul = jnp.dot(x, w1l_s[...], preferred_element_type=F32) + b1l_ref[0]
        x_glu = jnp.minimum(ug, SWIGLU_LIMIT)
        x_lin = jnp.clip(ul, -SWIGLU_LIMIT, SWIGLU_LIMIT)
        act = x_glu * jax.nn.sigmoid(SWIGLU_ALPHA * x_glu) * (x_lin + 1.0)
        out = jnp.dot(act.astype(BF16), w2_s[...], preferred_element_type=F32) + b2_ref[0]
        o_ref[...] = out.reshape(o_ref.shape)

    @pl.when(i >= nu_ref[0])
    def _():
        o_ref[...] = jnp.zeros(o_ref.shape, F32)


def _experts(block_exp, n_used, xs, w1, b1g, b1l, w2, b2):
    cap = xs.shape[0]
    dm = w2.shape[2]
    dff = w2.shape[1]
    n_blocks = cap // MOE_BLOCK
    rows_blk = (MOE_BLOCK,) + xs.shape[1:]
    row_in = lambda i, be, nu: (jnp.minimum(i, nu[0] - 1), 0, 0)
    row_out = lambda i, be, nu: (i, 0, 0)
    wsel = lambda i, be, nu: (be[i], 0, 0)
    grid_spec = pltpu.PrefetchScalarGridSpec(
        num_scalar_prefetch=2,
        grid=(n_blocks,),
        in_specs=[pl.BlockSpec(rows_blk, row_in),
                  pl.BlockSpec((1, dm, 2 * dff), wsel),
                  pl.BlockSpec((1, 1, dff), wsel), pl.BlockSpec((1, 1, dff), wsel),
                  pl.BlockSpec((1, dff, dm), wsel), pl.BlockSpec((1, 1, dm), wsel)],
        out_specs=pl.BlockSpec(rows_blk, row_out),
        scratch_shapes=[pltpu.VMEM((dm, dff), BF16), pltpu.VMEM((dm, dff), BF16), pltpu.VMEM((dff, dm), BF16)],
    )
    return pl.pallas_call(
        _expert_kernel,
        grid_spec=grid_spec,
        out_shape=jax.ShapeDtypeStruct(xs.shape, F32),
        compiler_params=_cparams(("arbitrary",)),
        name="moe_experts",
    )(block_exp, n_used, xs, w1, b1g, b1l, w2, b2)


def _combine_kernel(dest_ref, gate_ref, h1_ref, gf_ref, eo_ref, y_ref, buf, sem):
    tm = h1_ref.shape[0]

    def issue(g, c):
        for u in range(DMA_ISSUE_UNROLL):
            r = g * DMA_ISSUE_UNROLL + u
            for k in range(TOP_K):
                _row_copy(eo_ref, dest_ref[r * TOP_K + k], buf.at[k], r, sem).start(priority=(u * TOP_K + k) % 2)
        return c

    lax.fori_loop(0, tm // DMA_ISSUE_UNROLL, issue, 0)
    for k in range(TOP_K):
        pltpu.make_async_copy(eo_ref.at[pl.ds(0, tm)], buf.at[k], sem).wait()
    gate = gate_ref[...]
    y = h1_ref[...]
    for k in range(TOP_K):
        y = y + gate[:, k:k + 1] * buf[k].reshape(y.shape)
    y_ref[...] = _rms(y, gf_ref[...])


def _combine(dest_flat, gate, h1, gf, eo):
    t, dm = h1.shape
    tm = _row_tile(t, 128)
    row = lambda i: (i, 0)
    return pl.pallas_call(
        _combine_kernel,
        grid=(t // tm,),
        in_specs=[pl.BlockSpec((tm * TOP_K,), lambda i: (i,), memory_space=pltpu.SMEM),
                  pl.BlockSpec((tm, V7X_LANES), row), pl.BlockSpec((tm, dm), row),
                  pl.BlockSpec((1, dm), lambda i: (0, 0)), pl.BlockSpec(memory_space=pl.ANY)],
        out_specs=pl.BlockSpec((tm, dm), row),
        out_shape=jax.ShapeDtypeStruct((t, dm), F32),
        scratch_shapes=[pltpu.VMEM((TOP_K, tm) + eo.shape[1:], F32), pltpu.SemaphoreType.DMA(())],
        compiler_params=_cparams(("arbitrary",)),
        name="moe_combine",
    )(dest_flat, gate, h1, gf, eo)


def _moe_and_final_norm(parts, router_b_unused, w1, b1, w2, b2, normf_g):
    del router_b_unused
    sizes = [p[0].shape[0] for p in parts]
    t = sum(sizes)
    dm = parts[0][0].shape[1]
    logits = jnp.concatenate([p[2] for p in parts], axis=0)
    idx, gate, rank, counts = _route(logits)
    cnt = counts[0, :N_EXPERTS].astype(I32)
    padded = ((cnt + MOE_BLOCK - 1) // MOE_BLOCK) * MOE_BLOCK
    pad_ends = jnp.cumsum(padded)
    pad_starts = pad_ends - padded
    n_blocks = -(-(t * TOP_K + N_EXPERTS * (MOE_BLOCK - 1)) // MOE_BLOCK)
    cap = n_blocks * MOE_BLOCK
    block_start = jnp.arange(n_blocks, dtype=I32) * MOE_BLOCK
    block_exp = jnp.minimum(jnp.sum(pad_ends[None, :] <= block_start[:, None], axis=1), N_EXPERTS - 1).astype(I32)
    n_used = (pad_ends[-1:] // MOE_BLOCK).astype(I32)
    dest = (pad_starts[idx[:, :TOP_K]] + rank[:, :TOP_K]).astype(I32).reshape(-1)

    xs = jnp.zeros((cap,) + _row_tiles(dm), F32)
    off = 0
    for (h1, hn, _), n in zip(parts, sizes):
        xs = _dispatch(dest[off * TOP_K:(off + n) * TOP_K], hn, xs)
        off += n
    b1g = b1[:, None, 0::2]
    b1l = b1[:, None, 1::2]
    eo = _experts(block_exp, n_used, xs, w1, b1g, b1l, w2, b2[:, None, :])
    outs = []
    off = 0
    for (h1, hn, _), n in zip(parts, sizes):
        outs.append(_combine(dest[off * TOP_K:(off + n) * TOP_K], gate[off:off + n], h1, normf_g, eo))
        off += n
    return outs


def _rope_tables(pos, half):
    inv = ROPE_BASE ** (-jnp.arange(half, dtype=F32) / half)
    ang = pos.astype(F32)[:, None] * inv[None, :]
    return jnp.cos(ang), jnp.sin(ang)


def kernel(x_prompt, x_sample, cache_k, cache_v, state_ret, cache_mem_k, cache_mem_v, page_table, mem_prompt,
           norm1_g, w_in, lambda_q1, lambda_k1, lambda_q2, lambda_k2, subln_g, norm_mem_g, w_mem_kv, w_branch,
           w_out, norm2_g, router_w, router_b, w1, b1, w2, b2, normf_g):
    b, s, dm = x_prompt.shape
    db, t, _ = x_sample.shape
    depth = w_in.shape[0]
    assert depth == 1, "final norm is fused into the last layer's MoE combine; one layer supported"
    assert b == 1
    page = cache_k.shape[2]
    past_len = page_table.shape[1] * page
    dk_ret = dm // H_RET
    n_mem = mem_prompt.shape[1]

    cos_p, sin_p = _rope_tables(jnp.arange(s), dk_ret // 2)
    cos_s, sin_s = _rope_tables(past_len + jnp.arange(t), dk_ret // 2)
    cos_s = jnp.tile(cos_s, (db, 1))
    sin_s = jnp.tile(sin_s, (db, 1))

    l = 0
    lam_init = 0.8 - 0.6 * math.exp(-0.3 * l)
    lam4 = jnp.stack([lambda_q1[l], lambda_k1[l], lambda_q2[l], lambda_k2[l]]).astype(F32)
    g1 = norm1_g[l][None, :]
    sub_g = subln_g[l][None, :]
    d3 = 3 * dm
    w_l = w_in[l]
    w_diff = w_l[:, :d3].astype(BF16)
    w_ret = w_l[:, d3:d3 + 4 * dm].astype(BF16)
    w_mg = w_l[:, d3 + 4 * dm:].astype(BF16)
    wb = w_branch[l].astype(BF16)
    wo = w_out[l].astype(BF16)
    g2 = norm2_g[l][None, :]
    rw = jnp.zeros((dm, V7X_LANES), F32).at[:, :N_EXPERTS].set(router_w[l]).astype(BF16)
    rb = jnp.full((1, V7X_LANES), NEG_INF, F32).at[0, :N_EXPERTS].set(router_b[l])

    xp = x_prompt.reshape(s, dm)
    q_d, k_f, v_f, k_b, v_t = _proj_diff(xp, g1, w_diff, BF16)
    y_d = _diff_attn_prompt(q_d, k_b, v_t, lam4, subln_g[l][:, None], lam_init)
    rq, rk, rv, rg = _proj_ret(xp, g1, w_ret, cos_p, sin_p)
    to3 = lambda a: a.reshape(1, s, dm)
    y_r, st_p = _retention(to3(rq), to3(rk), to3(rv), to3(rg), jnp.zeros((1, H_RET, dk_ret, dk_ret), F32),
                           chunk=_row_tile(s, RET_CHUNK), y_dtype=BF16, mm_dtype=BF16)
    mq, gt = _proj_mem_gate(xp, g1, w_mg, BF16)
    mkv = _mem_kv(mem_prompt.reshape(n_mem, dm), norm_mem_g[l][None, :], w_mem_kv[l].astype(BF16))
    mk, mv = mkv[:, :dm], mkv[:, dm:]
    y_m = _mem_attn(mq[None], mk[None], mv[None], BF16)
    part_p = _merge(xp, y_d, y_r.reshape(s, dm), y_m.reshape(s, dm), gt, wb, wo, g2, rw, rb)

    ns = db * t
    xs_ = x_sample.reshape(ns, dm)
    q_s, ks_f, vs_f, _, _ = _proj_diff(xs_, g1, w_diff, F32)
    k_s5 = ks_f.reshape(db, t, H_DIFF, D_V)
    v_s5 = vs_f.reshape(db, t, H_DIFF, D_V)
    y_ds = _diff_attn_sample(page_table, q_s.reshape(db, t, dm), k_s5, v_s5, cache_k, cache_v, l,
                             lam4, sub_g, lam_init)
    rq, rk, rv, rg = _proj_ret(xs_, g1, w_ret, cos_s, sin_s)
    tos = lambda a: a.reshape(db, t, dm)
    y_rs, st_s = _retention(tos(rq), tos(rk), tos(rv), tos(rg), state_ret[l].astype(F32),
                            chunk=t, y_dtype=F32, mm_dtype=F32)
    mq_s, gt_s = _proj_mem_gate(xs_, g1, w_mg, F32)
    y_ms = _mem_attn(mq_s.reshape(db, t, dm), cache_mem_k[l].reshape(db, n_mem, dm),
                     cache_mem_v[l].reshape(db, n_mem, dm), F32)
    part_s = _merge(xs_, y_ds.reshape(ns, dm), y_rs.reshape(ns, dm), y_ms.reshape(ns, dm), gt_s, wb, wo, g2, rw, rb)

    y_p, y_s = _moe_and_final_norm([part_p, part_s], None, w1[l], b1[l], w2[l], b2[l], normf_g[None, :])

    return (y_p.reshape(b, s, dm), y_s.reshape(db, t, dm),
            k_f.reshape(1, b, s, H_DIFF, D_V), v_f.reshape(1, b, s, H_DIFF, D_V),
            st_p.reshape(1, b, H_RET, dk_ret, dk_ret),
            mk.reshape(1, b, n_mem, H_MEM, dm // H_MEM), mv.reshape(1, b, n_mem, H_MEM, dm // H_MEM),
            k_s5[None], v_s5[None], st_s[None].astype(state_ret.dtype))
```

```python
import functools
import math

import jax
import jax.numpy as jnp
import numpy as np
from jax import lax
from jax.experimental import pallas as pl
from jax.experimental.pallas import tpu as pltpu

F32 = jnp.float32
BF16 = jnp.bfloat16
I32 = jnp.int32

H_DIFF = 8
D_QK = 64
D_V = 128
H_RET = 4
H_MEM = 4
N_BRANCH = 3
N_EXPERTS = 32
TOP_K = 4
SWIGLU_LIMIT = 7.0
SWIGLU_ALPHA = 1.702
ROPE_BASE = 10000.0
RMS_EPS = 1e-6

V7X_LANES = 128
V7X_VMEM_LIMIT_BYTES = 56 * 1024 * 1024

MOE_BLOCK = 256
DMA_ISSUE_UNROLL = 16
MOE_ROW_DMA_TOKENS = 256
SAMPLE_PAGES_PER_STEP = 16
SAMPLE_PAGES_PER_UPDATE = 4
RET_CHUNK = 256
NEG_INF = float("-inf")
DIFF_Q_SCALE = (D_QK ** -0.5) * math.log2(math.e)


def _cparams(sem):
    return pltpu.CompilerParams(dimension_semantics=sem, vmem_limit_bytes=V7X_VMEM_LIMIT_BYTES)


def _rms(x, g=None):
    y = x * lax.rsqrt(jnp.mean(x * x, axis=-1, keepdims=True) + RMS_EPS)
    return y if g is None else y * g


def _row_tiles(dm):
    return (dm // V7X_LANES, V7X_LANES)


def _row_tile(n, pref):
    t = min(n, pref)
    while n % t:
        t //= 2
    return t


def _proj_diff_kernel(x_ref, g_ref, w_ref, q_ref, kf_ref, vf_ref, kb_ref, vt_ref):
    xn = _rms(x_ref[...], g_ref[...]).astype(BF16)
    d = kf_ref.shape[-1]
    q = jnp.dot(xn, w_ref[:, 0:d], preferred_element_type=F32)
    q_ref[...] = (q * DIFF_Q_SCALE).astype(q_ref.dtype)
    k = jnp.dot(xn, w_ref[:, d:2 * d], preferred_element_type=F32)
    kf_ref[...] = k
    kb_ref[...] = k.astype(BF16)
    v = jnp.dot(xn, w_ref[:, 2 * d:3 * d], preferred_element_type=F32)
    vf_ref[...] = v
    vt_ref[...] = v.T.astype(BF16)


def _proj_diff(x, g, w, q_dtype):
    t, dm = x.shape
    tm = _row_tile(t, 512)
    row = lambda i: (i, 0)
    blk = pl.BlockSpec((tm, dm), row)
    return pl.pallas_call(
        _proj_diff_kernel,
        grid=(t // tm,),
        in_specs=[blk, pl.BlockSpec((1, dm), lambda i: (0, 0)), pl.BlockSpec(w.shape, lambda i: (0, 0))],
        out_specs=[blk] * 4 + [pl.BlockSpec((dm, tm), lambda i: (0, i))],
        out_shape=[jax.ShapeDtypeStruct((t, dm), q_dtype), jax.ShapeDtypeStruct((t, dm), F32),
                   jax.ShapeDtypeStruct((t, dm), F32), jax.ShapeDtypeStruct((t, dm), BF16),
                   jax.ShapeDtypeStruct((dm, t), BF16)],
        compiler_params=_cparams(("parallel",)),
        name="proj_diff",
    )(x, g, w)


def _proj_ret_kernel(x_ref, g_ref, w_ref, cos_ref, sin_ref, q_ref, k_ref, v_ref, gate_ref):
    xn = _rms(x_ref[...], g_ref[...]).astype(BF16)
    d = q_ref.shape[-1]
    dk = d // H_RET
    half = dk // 2
    cos = cos_ref[...]
    sin = sin_ref[...]

    def rope(u, scale):
        outs = []
        for h in range(H_RET):
            x1 = u[:, h * dk:h * dk + half]
            x2 = u[:, h * dk + half:(h + 1) * dk]
            outs.append((x1 * cos - x2 * sin) * scale)
            outs.append((x1 * sin + x2 * cos) * scale)
        return jnp.concatenate(outs, axis=-1)

    q_ref[...] = rope(jnp.dot(xn, w_ref[:, 0:d], preferred_element_type=F32), 1.0)
    k_ref[...] = rope(jnp.dot(xn, w_ref[:, d:2 * d], preferred_element_type=F32), dk ** -0.5)
    v_ref[...] = jnp.dot(xn, w_ref[:, 2 * d:3 * d], preferred_element_type=F32)
    gate_ref[...] = jnp.dot(xn, w_ref[:, 3 * d:4 * d], preferred_element_type=F32)


def _proj_ret(x, g, w, cos, sin):
    t, dm = x.shape
    tm = _row_tile(t, 512)
    row = lambda i: (i, 0)
    blk = pl.BlockSpec((tm, dm), row)
    tab = pl.BlockSpec((tm, cos.shape[1]), row)
    return pl.pallas_call(
        _proj_ret_kernel,
        grid=(t // tm,),
        in_specs=[blk, pl.BlockSpec((1, dm), lambda i: (0, 0)), pl.BlockSpec(w.shape, lambda i: (0, 0)), tab, tab],
        out_specs=[blk] * 4,
        out_shape=[jax.ShapeDtypeStruct((t, dm), F32)] * 4,
        compiler_params=_cparams(("parallel",)),
        name="proj_ret",
    )(x, g, w, cos, sin)


def _proj_mem_gate_kernel(x_ref, g_ref, w_ref, mq_ref, gt_ref):
    xn = _rms(x_ref[...], g_ref[...]).astype(BF16)
    d = mq_ref.shape[-1]
    mq = jnp.dot(xn, w_ref[:, 0:d], preferred_element_type=F32)
    mq_ref[...] = (mq * ((d // H_MEM) ** -0.5)).astype(mq_ref.dtype)
    gt_ref[...] = jnp.dot(xn, w_ref[:, d:], preferred_element_type=F32)


def _proj_mem_gate(x, g, w, mq_dtype):
    t, dm = x.shape
    tm = _row_tile(t, 512)
    row = lambda i: (i, 0)
    return pl.pallas_call(
        _proj_mem_gate_kernel,
        grid=(t // tm,),
        in_specs=[pl.BlockSpec((tm, dm), row), pl.BlockSpec((1, dm), lambda i: (0, 0)),
                  pl.BlockSpec(w.shape, lambda i: (0, 0))],
        out_specs=[pl.BlockSpec((tm, dm), row), pl.BlockSpec((tm, N_BRANCH * dm), row)],
        out_shape=[jax.ShapeDtypeStruct((t, dm), mq_dtype), jax.ShapeDtypeStruct((t, N_BRANCH * dm), F32)],
        compiler_params=_cparams(("parallel",)),
        name="proj_mem_gate",
    )(x, g, w)


def _lambda_value(lam_ref, lam_init):
    t = lam_ref[...]
    a = jnp.sum(t[0:1] * t[1:2], axis=-1, keepdims=True)
    b = jnp.sum(t[2:3] * t[3:4], axis=-1, keepdims=True)
    return jnp.exp(a) - jnp.exp(b) + lam_init


def _diff_attn_prompt_kernel(q_ref, k_ref, vt_ref, lam_ref, g_ref, o_ref, m_s, l_s, acc_s, sa_s, sb_s, mba_s, mbb_s,
                             *, tq, tk, rc, lam_init):
    qi = pl.program_id(1)
    q = q_ref[...]
    lane = lax.broadcasted_iota(I32, q.shape, 1)
    zero = jnp.zeros_like(q)
    qz = jnp.concatenate([jnp.where(lane < D_QK, q, zero), jnp.where(lane >= D_QK, q, zero)], axis=0)
    m_s[...] = jnp.full(m_s.shape, NEG_INF, F32)
    l_s[...] = jnp.zeros(l_s.shape, F32)
    acc_s[...] = jnp.zeros(acc_s.shape, F32)
    n_full = (qi * tq) // tk

    def scores(j, masked, s_buf, mb_buf):
        ks = k_ref[pl.ds(pl.multiple_of(j * tk, tk), tk), :]
        s = lax.dot_general(ks, qz, (((1,), (1,)), ((), ())), preferred_element_type=F32)
        if masked:
            kpos = lax.broadcasted_iota(I32, s.shape, 0) + j * tk
            col = lax.broadcasted_iota(I32, s.shape, 1)
            qpos = jnp.where(col >= tq, col - tq, col) + qi * tq
            s = jnp.where(kpos <= qpos, s, NEG_INF)
        s_buf[...] = s
        mb_buf[...] = jnp.max(s, axis=0, keepdims=True)

    def absorb(s_buf, mb_buf, vblock):
        m_old = m_s[...]
        m_new = jnp.maximum(m_old, mb_buf[...])
        alpha = jnp.exp2(m_old - m_new)
        part = jnp.zeros((8, 2 * tq), F32)
        vstart = pl.multiple_of(vblock * tk, tk)
        pv = None
        for r in range(tk // rc):
            p = jnp.exp2(s_buf[r * rc:(r + 1) * rc, :] - m_new)
            part = part + jnp.sum(p.reshape(rc // 8, 8, 2 * tq), axis=0)
            vt = vt_ref[:, pl.ds(vstart + r * rc, rc)]
            d = jnp.dot(vt, p.astype(BF16), preferred_element_type=F32)
            pv = d if pv is None else pv + d
        l_s[...] = alpha * l_s[...] + jnp.sum(part, axis=0, keepdims=True)
        acc_s[...] = alpha * acc_s[...] + pv
        m_s[...] = m_new

    scores(n_full, True, sa_s, mba_s)
    n_pairs = n_full // 2

    def body(i, c):
        scores(2 * i, False, sb_s, mbb_s)
        absorb(sa_s, mba_s, jnp.where(i == 0, n_full, 2 * i - 1))
        scores(2 * i + 1, False, sa_s, mba_s)
        absorb(sb_s, mbb_s, 2 * i)
        return c

    lax.fori_loop(0, n_pairs, body, 0)
    pending = jnp.where(n_pairs == 0, n_full, 2 * n_pairs - 1)

    @pl.when(n_full % 2 == 1)
    def _():
        scores(n_full - 1, False, sb_s, mbb_s)
        absorb(sa_s, mba_s, pending)
        absorb(sb_s, mbb_s, n_full - 1)

    @pl.when(n_full % 2 == 0)
    def _():
        absorb(sa_s, mba_s, pending)

    o = acc_s[...] / l_s[...]
    lam = _lambda_value(lam_ref, lam_init)
    od = o[:, :tq] - lam * o[:, tq:]
    y = od * lax.rsqrt(jnp.mean(od * od, axis=0, keepdims=True) + RMS_EPS) * g_ref[...] * (1.0 - lam_init)
    o_ref[...] = y.T.astype(o_ref.dtype)


def _diff_attn_prompt(q, k, vt, lam4, subln_g_col, lam_init):
    s, dm = q.shape
    tq = _row_tile(s, 512)
    tk = _row_tile(s, 1024)
    rc = _row_tile(tk, 256)
    kern = functools.partial(_diff_attn_prompt_kernel, tq=tq, tk=tk, rc=rc, lam_init=lam_init)
    row_stat = pltpu.VMEM((1, 2 * tq), F32)
    score_buf = pltpu.VMEM((tk, 2 * tq), F32)
    return pl.pallas_call(
        kern,
        grid=(H_DIFF, s // tq),
        in_specs=[pl.BlockSpec((tq, D_V), lambda h, i: (i, h)),
                  pl.BlockSpec((s, D_V), lambda h, i: (0, h)),
                  pl.BlockSpec((D_V, s), lambda h, i: (h, 0)),
                  pl.BlockSpec(lam4.shape, lambda h, i: (0, 0)),
                  pl.BlockSpec((D_V, 1), lambda h, i: (0, 0))],
        out_specs=pl.BlockSpec((tq, D_V), lambda h, i: (i, h)),
        out_shape=jax.ShapeDtypeStruct((s, dm), BF16),
        scratch_shapes=[row_stat, row_stat, pltpu.VMEM((D_V, 2 * tq), F32),
                        score_buf, score_buf, row_stat, row_stat],
        compiler_params=_cparams(("parallel", "parallel")),
        name="diff_attn_prompt",
    )(q, k, vt, lam4, subln_g_col)


def _diff_attn_sample_kernel(pt_ref, q_ref, kn_ref, vn_ref, lam_ref, g_ref, *rest, pps, lam_init):
    k_refs = rest[:pps]
    v_refs = rest[pps:2 * pps]
    o_ref, qmat_s, bias_s, m_s, l_s, acc_s, s_s = rest[2 * pps:]
    j = pl.program_id(1)
    nq = q_ref.shape[1]
    pc = bias_s.shape[1]

    @pl.when(j == 0)
    def _():
        qb = q_ref[0]
        lane = lax.broadcasted_iota(I32, (nq, D_V), 1)
        parts = []
        for h in range(H_DIFF):
            blk = qb[:, h * D_V:(h + 1) * D_V]
            parts.append(jnp.where(lane < D_QK, blk, 0.0))
            parts.append(jnp.where(lane >= D_QK, blk, 0.0))
        qmat_s[...] = jnp.concatenate(parts, axis=0).astype(BF16)
        r = lax.broadcasted_iota(I32, bias_s.shape, 0)
        c = lax.broadcasted_iota(I32, bias_s.shape, 1)
        bias_s[...] = jnp.where((c % H_DIFF) == (r // (2 * nq)), 0.0, NEG_INF)
        m_s[...] = jnp.full(m_s.shape, NEG_INF, F32)
        l_s[...] = jnp.zeros(l_s.shape, F32)
        acc_s[...] = jnp.zeros(acc_s.shape, F32)

    def flat(page):
        return page.reshape(page.shape[0] * page.shape[1], page.shape[2]).astype(BF16)

    qmat = qmat_s[...]
    grp = s_s.shape[1] // pc
    for g0 in range(0, pps, grp):
        mb = None
        for r in range(grp):
            s = lax.dot_general(qmat, flat(k_refs[g0 + r][0, 0]), (((1,), (1,)), ((), ())),
                                preferred_element_type=F32) + bias_s[...]
            s_s[:, r * pc:(r + 1) * pc] = s
            mr = jnp.max(s, axis=-1, keepdims=True)
            mb = mr if mb is None else jnp.maximum(mb, mr)
        m_old = m_s[...]
        m_new = jnp.maximum(m_old, mb)
        alpha = jnp.exp2(m_old - m_new)
        lsum = None
        pv = None
        for r in range(grp):
            p = jnp.exp2(s_s[:, r * pc:(r + 1) * pc] - m_new)
            ls = jnp.sum(p, axis=-1, keepdims=True)
            lsum = ls if lsum is None else lsum + ls
            d = jnp.dot(p.astype(BF16), flat(v_refs[g0 + r][0, 0]), preferred_element_type=F32)
            pv = d if pv is None else pv + d
        l_s[...] = alpha * l_s[...] + lsum
        acc_s[...] = alpha * acc_s[...] + pv
        m_s[...] = m_new

    @pl.when(j == pl.num_programs(1) - 1)
    def _():
        s = lax.dot_general(qmat, flat(kn_ref[0]), (((1,), (1,)), ((), ())), preferred_element_type=F32)
        r = lax.broadcasted_iota(I32, s.shape, 0)
        c = lax.broadcasted_iota(I32, s.shape, 1)
        keep = ((c % H_DIFF) == (r // (2 * nq))) & ((c // H_DIFF) <= (r % nq))
        s = jnp.where(keep, s, NEG_INF)
        m_o = m_s[...]
        m_n = jnp.maximum(m_o, jnp.max(s, axis=-1, keepdims=True))
        al = jnp.exp2(m_o - m_n)
        p = jnp.exp2(s - m_n)
        l = al * l_s[...] + jnp.sum(p, axis=-1, keepdims=True)
        acc = al * acc_s[...] + jnp.dot(p.astype(BF16), flat(vn_ref[0]), preferred_element_type=F32)
        o = acc / l
        lam = _lambda_value(lam_ref, lam_init)
        g = g_ref[...]
        outs = []
        for h in range(H_DIFF):
            base = h * 2 * nq
            od = o[base:base + nq] - lam * o[base + nq:base + 2 * nq]
            outs.append(_rms(od, g) * (1.0 - lam_init))
        o_ref[0] = jnp.concatenate(outs, axis=-1)


def _diff_attn_sample(page_table, q, k_new, v_new, cache_k, cache_v, layer, lam4, subln_g, lam_init,
                      pps=SAMPLE_PAGES_PER_STEP, grp=SAMPLE_PAGES_PER_UPDATE):
    b, nq, dm = q.shape
    n_pages = page_table.shape[1]
    while n_pages % pps:
        pps //= 2
    grp = min(grp, pps)
    page_shape = (1, 1) + cache_k.shape[2:]
    page_specs = [pl.BlockSpec(page_shape, functools.partial(
        lambda bi, j, pt, r: (layer, pt[bi, j * pps + r], 0, 0, 0), r=r)) for r in range(pps)]
    rows = H_DIFF * 2 * nq
    cols = cache_k.shape[2] * cache_k.shape[3]
    kern = functools.partial(_diff_attn_sample_kernel, pps=pps, lam_init=lam_init)
    per_seq3 = lambda bi, j, pt: (bi, 0, 0)
    per_seq4 = lambda bi, j, pt: (bi, 0, 0, 0)
    const2 = lambda bi, j, pt: (0, 0)
    grid_spec = pltpu.PrefetchScalarGridSpec(
        num_scalar_prefetch=1,
        grid=(b, n_pages // pps),
        in_specs=[pl.BlockSpec((1, nq, dm), per_seq3),
                  pl.BlockSpec((1,) + k_new.shape[1:], per_seq4),
                  pl.BlockSpec((1,) + v_new.shape[1:], per_seq4),
                  pl.BlockSpec(lam4.shape, const2),
                  pl.BlockSpec((1, D_V), const2)] + page_specs + page_specs,
        out_specs=pl.BlockSpec((1, nq, dm), per_seq3),
        scratch_shapes=[pltpu.VMEM((rows, D_V), BF16), pltpu.VMEM((rows, cols), F32),
                        pltpu.VMEM((rows, 1), F32), pltpu.VMEM((rows, 1), F32), pltpu.VMEM((rows, D_V), F32),
                        pltpu.VMEM((rows, grp * cols), F32)],
    )
    return pl.pallas_call(
        kern,
        grid_spec=grid_spec,
        out_shape=jax.ShapeDtypeStruct((b, nq, dm), F32),
        compiler_params=_cparams(("parallel", "arbitrary")),
        name="diff_attn_sample",
    )(page_table, q, k_new, v_new, lam4, subln_g, *([cache_k] * pps), *([cache_v] * pps))


def _retention_log_decay(h):
    return float(np.log(np.float32(1.0) - np.exp2(np.float32(-5.0 - h))))


def _retention_kernel(q_ref, k_ref, v_ref, gate_ref, s0_ref, y_ref, sout_ref, state_s, *, mm_dtype):
    c = pl.program_id(1)

    @pl.when(c == 0)
    def _():
        state_s[...] = s0_ref[0]

    cl = q_ref.shape[1]
    dk = state_s.shape[1]
    ii = lax.broadcasted_iota(I32, (cl, cl), 0)
    jj = lax.broadcasted_iota(I32, (cl, cl), 1)
    dist = (ii - jj).astype(F32)
    ri = lax.broadcasted_iota(I32, (cl, 1), 0).astype(F32)
    for h in range(H_RET):
        log_g = _retention_log_decay(h)
        sl = slice(h * dk, (h + 1) * dk)
        q = q_ref[0, :, sl]
        k = k_ref[0, :, sl]
        decay = jnp.where(dist >= 0, jnp.exp(jnp.maximum(dist, 0.0) * log_g), 0.0)
        row_decay = jnp.exp((ri + 1.0) * log_g)
        col_decay = jnp.exp((cl - 1.0 - ri) * log_g)
        qm = q.astype(mm_dtype)
        vm = v_ref[0, :, sl].astype(mm_dtype)
        state = state_s[h]
        sc = lax.dot_general(qm, k.astype(mm_dtype), (((1,), (1,)), ((), ())), preferred_element_type=F32) * decay
        o = jnp.dot(sc.astype(mm_dtype), vm, preferred_element_type=F32)
        o = o + jnp.dot(qm, state.astype(mm_dtype), preferred_element_type=F32) * row_decay
        kw = (k * col_decay).astype(mm_dtype)
        upd = lax.dot_general(kw, vm, (((0,), (0,)), ((), ())), preferred_element_type=F32)
        state_s[h] = math.exp(cl * log_g) * state + upd
        g = gate_ref[0, :, sl]
        y_ref[0, :, sl] = (g * jax.nn.sigmoid(g) * _rms(o)).astype(y_ref.dtype)

    @pl.when(c == pl.num_programs(1) - 1)
    def _():
        sout_ref[0] = state_s[...]


def _retention(q, k, v, gate, state0, chunk, y_dtype, mm_dtype):
    b, s, dm = q.shape
    nc = s // chunk
    blk = pl.BlockSpec((1, chunk, dm), lambda bi, c: (bi, c, 0))
    st = pl.BlockSpec((1,) + state0.shape[1:], lambda bi, c: (bi, 0, 0, 0))
    return pl.pallas_call(
        functools.partial(_retention_kernel, mm_dtype=mm_dtype),
        grid=(b, nc),
        in_specs=[blk, blk, blk, blk, st],
        out_specs=[blk, st],
        out_shape=[jax.ShapeDtypeStruct((b, s, dm), y_dtype), jax.ShapeDtypeStruct(state0.shape, F32)],
        scratch_shapes=[pltpu.VMEM(state0.shape[1:], F32)],
        compiler_params=_cparams(("parallel", "arbitrary")),
        name="retention",
    )(q, k, v, gate, state0)


def _mem_kv_kernel(mem_ref, g_ref, w_ref, o_ref):
    xn = _rms(mem_ref[...], g_ref[...]).astype(BF16)
    o_ref[...] = jnp.dot(xn, w_ref[...], preferred_element_type=F32)


def _mem_kv(mem, g, w):
    m, dm = mem.shape
    n = w.shape[1]
    return pl.pallas_call(
        _mem_kv_kernel,
        grid=(n // dm,),
        in_specs=[pl.BlockSpec((m, dm), lambda i: (0, 0)), pl.BlockSpec((1, dm), lambda i: (0, 0)),
                  pl.BlockSpec((dm, dm), lambda i: (0, i))],
        out_specs=pl.BlockSpec((m, dm), lambda i: (0, i)),
        out_shape=jax.ShapeDtypeStruct((m, n), F32),
        compiler_params=_cparams(("parallel",)),
        name="mem_kv",
    )(mem, g, w)


def _mem_attn_kernel(q_ref, mk_ref, mv_ref, o_ref):
    q = q_ref[0].astype(BF16)
    mk = mk_ref[0].astype(BF16)
    mv = mv_ref[0].astype(BF16)
    d = q.shape[-1] // H_MEM
    outs = []
    for h in range(H_MEM):
        sl = slice(h * d, (h + 1) * d)
        s = lax.dot_general(q[:, sl], mk[:, sl], (((1,), (1,)), ((), ())), preferred_element_type=F32)
        m = jnp.max(s, axis=-1, keepdims=True)
        p = jnp.exp(s - m)
        p = p / jnp.sum(p, axis=-1, keepdims=True)
        outs.append(jnp.dot(p.astype(BF16), mv[:, sl], preferred_element_type=F32))
    o_ref[0] = jnp.concatenate(outs, axis=-1).astype(o_ref.dtype)


def _mem_attn(q, mk, mv, y_dtype):
    b, t, dm = q.shape
    tm = _row_tile(t, 512)
    qb = pl.BlockSpec((1, tm, dm), lambda bi, i: (bi, i, 0))
    mb = pl.BlockSpec((1,) + mk.shape[1:], lambda bi, i: (bi, 0, 0))
    return pl.pallas_call(
        _mem_attn_kernel,
        grid=(b, t // tm),
        in_specs=[qb, mb, mb],
        out_specs=qb,
        out_shape=jax.ShapeDtypeStruct((b, t, dm), y_dtype),
        compiler_params=_cparams(("parallel", "parallel")),
        name="mem_attn",
    )(q, mk, mv)


def _merge_kernel(x_ref, yd_ref, yr_ref, ym_ref, gt_ref, wb_ref, wo_ref, g2_ref, rw_ref, rb_ref,
                  h1_ref, hn_ref, lg_ref):
    dm = x_ref.shape[-1]
    mixed = jnp.zeros(x_ref.shape, F32)
    for i, y_ref in enumerate((yd_ref, yr_ref, ym_ref)):
        proj = jnp.dot(y_ref[...].astype(BF16), wb_ref[i], preferred_element_type=F32)
        mixed = mixed + jax.nn.sigmoid(gt_ref[:, i * dm:(i + 1) * dm]) * proj
    h1 = x_ref[...] + jnp.dot(mixed.astype(BF16), wo_ref[...], preferred_element_type=F32)
    h1_ref[...] = h1
    hn = _rms(h1, g2_ref[...])
    hn_ref[...] = hn.reshape(hn_ref.shape)
    lg_ref[...] = jnp.dot(hn.astype(BF16), rw_ref[...], preferred_element_type=F32) + rb_ref[...]


def _merge(x, yd, yr, ym, gt, wb, wo, g2, rw, rb):
    t, dm = x.shape
    tm = _row_tile(t, 512)
    row = lambda i: (i, 0)
    blk = pl.BlockSpec((tm, dm), row)
    c2 = lambda i: (0, 0)
    return pl.pallas_call(
        _merge_kernel,
        grid=(t // tm,),
        in_specs=[blk, blk, blk, blk, pl.BlockSpec((tm, N_BRANCH * dm), row),
                  pl.BlockSpec(wb.shape, lambda i: (0, 0, 0)), pl.BlockSpec(wo.shape, c2),
                  pl.BlockSpec((1, dm), c2), pl.BlockSpec(rw.shape, c2), pl.BlockSpec(rb.shape, c2)],
        out_specs=[blk, pl.BlockSpec((tm,) + _row_tiles(dm), lambda i: (i, 0, 0)), pl.BlockSpec((tm, V7X_LANES), row)],
        out_shape=[jax.ShapeDtypeStruct((t, dm), F32), jax.ShapeDtypeStruct((t,) + _row_tiles(dm), F32),
                   jax.ShapeDtypeStruct((t, V7X_LANES), F32)],
        compiler_params=_cparams(("parallel",)),
        name="merge",
    )(x, yd, yr, ym, gt, wb, wo, g2, rw, rb)


def _route_kernel(lg_ref, idx_ref, gate_ref, rank_ref, cnt_ref, carry_s):
    i = pl.program_id(0)

    @pl.when(i == 0)
    def _():
        carry_s[...] = jnp.zeros(carry_s.shape, F32)

    l = lg_ref[...]
    tm = l.shape[0]
    lane = lax.broadcasted_iota(I32, l.shape, 1)
    vals, idxs, hots = [], [], []
    for _ in range(TOP_K):
        m = jnp.max(l, axis=-1, keepdims=True)
        ik = jnp.min(jnp.where(l == m, lane, V7X_LANES), axis=-1, keepdims=True)
        hot = lane == ik
        vals.append(m)
        idxs.append(ik)
        hots.append(hot)
        l = jnp.where(hot, NEG_INF, l)
    es = [jnp.exp(v - vals[0]) for v in vals]
    den = es[0] + es[1] + es[2] + es[3]
    picked = jnp.zeros(l.shape, F32)
    for hot in hots:
        picked = picked + jnp.where(hot, 1.0, 0.0)
    r = lax.broadcasted_iota(I32, (tm, tm), 0)
    c = lax.broadcasted_iota(I32, (tm, tm), 1)
    before = jnp.where(c < r, 1.0, 0.0).astype(BF16)
    cum = jnp.dot(before, picked.astype(BF16), preferred_element_type=F32) + carry_s[0:1, :]
    idx_o = jnp.zeros(l.shape, I32)
    gate_o = jnp.zeros(l.shape, F32)
    rank_o = jnp.zeros(l.shape, I32)
    for k in range(TOP_K):
        rk = jnp.sum(jnp.where(hots[k], cum, 0.0), axis=-1, keepdims=True).astype(I32)
        idx_o = jnp.where(lane == k, idxs[k], idx_o)
        gate_o = jnp.where(lane == k, es[k] / den, gate_o)
        rank_o = jnp.where(lane == k, rk, rank_o)
    idx_ref[...] = idx_o
    gate_ref[...] = gate_o
    rank_ref[...] = rank_o
    carry_s[...] = carry_s[...] + jnp.sum(picked, axis=0, keepdims=True)
    cnt_ref[...] = carry_s[...]


def _route(logits):
    t = logits.shape[0]
    tm = _row_tile(t, 512)
    row = lambda i: (i, 0)
    blk = pl.BlockSpec((tm, V7X_LANES), row)
    return pl.pallas_call(
        _route_kernel,
        grid=(t // tm,),
        in_specs=[blk],
        out_specs=[blk, blk, blk, pl.BlockSpec((8, V7X_LANES), lambda i: (0, 0))],
        out_shape=[jax.ShapeDtypeStruct((t, V7X_LANES), I32), jax.ShapeDtypeStruct((t, V7X_LANES), F32),
                   jax.ShapeDtypeStruct((t, V7X_LANES), I32), jax.ShapeDtypeStruct((8, V7X_LANES), F32)],
        scratch_shapes=[pltpu.VMEM((8, V7X_LANES), F32)],
        compiler_params=_cparams(("arbitrary",)),
        name="route",
    )(logits)


def _row_copy(src, src_row, dst, dst_row, sem):
    return pltpu.make_async_copy(src.at[pl.ds(src_row, 1)], dst.at[pl.ds(dst_row, 1)], sem)


def _dispatch_kernel(dest_ref, hn_ref, xs_in_ref, xs_ref, sem):
    del xs_in_ref
    tm = hn_ref.shape[0]

    def issue(g, c):
        for u in range(DMA_ISSUE_UNROLL):
            r = g * DMA_ISSUE_UNROLL + u
            for k in range(TOP_K):
                _row_copy(hn_ref, r, xs_ref, dest_ref[r * TOP_K + k], sem).start(priority=(u * TOP_K + k) % 2)
        return c

    lax.fori_loop(0, tm // DMA_ISSUE_UNROLL, issue, 0)
    for k in range(TOP_K):
        pltpu.make_async_copy(hn_ref, xs_ref.at[pl.ds(0, tm)], sem).wait()


def _dispatch(dest_flat, hn, xs):
    t = hn.shape[0]
    tm = _row_tile(t, MOE_ROW_DMA_TOKENS)
    return pl.pallas_call(
        _dispatch_kernel,
        grid=(t // tm,),
        in_specs=[pl.BlockSpec((tm * TOP_K,), lambda i: (i,), memory_space=pltpu.SMEM),
                  pl.BlockSpec((tm,) + hn.shape[1:], lambda i: (i, 0, 0)),
                  pl.BlockSpec(memory_space=pl.ANY)],
        out_specs=pl.BlockSpec(memory_space=pl.ANY),
        out_shape=jax.ShapeDtypeStruct(xs.shape, xs.dtype),
        scratch_shapes=[pltpu.SemaphoreType.DMA(())],
        input_output_aliases={2: 0},
        compiler_params=_cparams(("arbitrary",)),
        name="moe_dispatch",
    )(dest_flat, hn, xs)


def _expert_kernel(be_ref, nu_ref, xs_ref, w1_ref, b1g_ref, b1l_ref, w2_ref, b2_ref, o_ref, w1g_s, w1l_s, w2_s):
    i = pl.program_id(0)
    e = be_ref[i]
    prev = be_ref[jnp.maximum(i - 1, 0)]

    @pl.when((i == 0) | (e != prev))
    def _():
        cw = 2 * V7X_LANES
        r = lax.broadcasted_iota(I32, (cw, cw), 0)
        c = lax.broadcasted_iota(I32, (cw, cw), 1)
        src_col = jnp.where(c < V7X_LANES, 2 * c, 2 * (c - V7X_LANES) + 1)
        sel = jnp.where(r == src_col, 1.0, 0.0).astype(BF16)
        for j in range(w1_ref.shape[2] // cw):
            t = jnp.dot(w1_ref[0, :, j * cw:(j + 1) * cw].astype(BF16), sel, preferred_element_type=F32)
            w1g_s[:, j * V7X_LANES:(j + 1) * V7X_LANES] = t[:, :V7X_LANES].astype(BF16)
            w1l_s[:, j * V7X_LANES:(j + 1) * V7X_LANES] = t[:, V7X_LANES:].astype(BF16)
        w2_s[...] = w2_ref[0].astype(BF16)

    @pl.when(i < nu_ref[0])
    def _():
        x = xs_ref[...].reshape(xs_ref.shape[0], w2_s.shape[1]).astype(BF16)
        ug = jnp.dot(x, w1g_s[...], preferred_element_type=F32) + b1g_ref[0]
        ul = jnp.dot(x, w1l_s[...], preferred_element_type=F32) + b1l_ref[0]
        x_glu = jnp.minimum(ug, SWIGLU_LIMIT)
        x_lin = jnp.clip(ul, -SWIGLU_LIMIT, SWIGLU_LIMIT)
        act = x_glu * jax.nn.sigmoid(SWIGLU_ALPHA * x_glu) * (x_lin + 1.0)
        out = jnp.dot(act.astype(BF16), w2_s[...], preferred_element_type=F32) + b2_ref[0]
        o_ref[...] = out.reshape(o_ref.shape)

    @pl.when(i >= nu_ref[0])
    def _():
        o_ref[...] = jnp.zeros(o_ref.shape, F32)


def _experts(block_exp, n_used, xs, w1, b1g, b1l, w2, b2):
    cap = xs.shape[0]
    dm = w2.shape[2]
    dff = w2.shape[1]
    n_blocks = cap // MOE_BLOCK
    rows_blk = (MOE_BLOCK,) + xs.shape[1:]
    row_in = lambda i, be, nu: (jnp.minimum(i, nu[0] - 1), 0, 0)
    row_out = lambda i, be, nu: (i, 0, 0)
    wsel = lambda i, be, nu: (be[i], 0, 0)
    grid_spec = pltpu.PrefetchScalarGridSpec(
        num_scalar_prefetch=2,
        grid=(n_blocks,),
        in_specs=[pl.BlockSpec(rows_blk, row_in),
                  pl.BlockSpec((1, dm, 2 * dff), wsel),
                  pl.BlockSpec((1, 1, dff), wsel), pl.BlockSpec((1, 1, dff), wsel),
                  pl.BlockSpec((1, dff, dm), wsel), pl.BlockSpec((1, 1, dm), wsel)],
        out_specs=pl.BlockSpec(rows_blk, row_out),
        scratch_shapes=[pltpu.VMEM((dm, dff), BF16), pltpu.VMEM((dm, dff), BF16), pltpu.VMEM((dff, dm), BF16)],
    )
    return pl.pallas_call(
        _expert_kernel,
        grid_spec=grid_spec,
        out_shape=jax.ShapeDtypeStruct(xs.shape, F32),
        compiler_params=_cparams(("arbitrary",)),
        name="moe_experts",
    )(block_exp, n_used, xs, w1, b1g, b1l, w2, b2)


def _combine_kernel(dest_ref, gate_ref, h1_ref, gf_ref, eo_ref, y_ref, buf, sem):
    tm = h1_ref.shape[0]

    def issue(g, c):
        for u in range(DMA_ISSUE_UNROLL):
            r = g * DMA_ISSUE_UNROLL + u
            for k in range(TOP_K):
                _row_copy(eo_ref, dest_ref[r * TOP_K + k], buf.at[k], r, sem).start(priority=(u * TOP_K + k) % 2)
        return c

    lax.fori_loop(0, tm // DMA_ISSUE_UNROLL, issue, 0)
    for k in range(TOP_K):
        pltpu.make_async_copy(eo_ref.at[pl.ds(0, tm)], buf.at[k], sem).wait()
    gate = gate_ref[...]
    y = h1_ref[...]
    for k in range(TOP_K):
        y = y + gate[:, k:k + 1] * buf[k].reshape(y.shape)
    y_ref[...] = _rms(y, gf_ref[...])


def _combine(dest_flat, gate, h1, gf, eo):
    t, dm = h1.shape
    tm = _row_tile(t, MOE_ROW_DMA_TOKENS)
    row = lambda i: (i, 0)
    return pl.pallas_call(
        _combine_kernel,
        grid=(t // tm,),
        in_specs=[pl.BlockSpec((tm * TOP_K,), lambda i: (i,), memory_space=pltpu.SMEM),
                  pl.BlockSpec((tm, V7X_LANES), row), pl.BlockSpec((tm, dm), row),
                  pl.BlockSpec((1, dm), lambda i: (0, 0)), pl.BlockSpec(memory_space=pl.ANY)],
        out_specs=pl.BlockSpec((tm, dm), row),
        out_shape=jax.ShapeDtypeStruct((t, dm), F32),
        scratch_shapes=[pltpu.VMEM((TOP_K, tm) + eo.shape[1:], F32), pltpu.SemaphoreType.DMA(())],
        compiler_params=_cparams(("arbitrary",)),
        name="moe_combine",
    )(dest_flat, gate, h1, gf, eo)


def _moe_and_final_norm(parts, router_b_unused, w1, b1, w2, b2, normf_g):
    del router_b_unused
    sizes = [p[0].shape[0] for p in parts]
    t = sum(sizes)
    dm = parts[0][0].shape[1]
    logits = jnp.concatenate([p[2] for p in parts], axis=0)
    idx, gate, rank, counts = _route(logits)
    cnt = counts[0, :N_EXPERTS].astype(I32)
    padded = ((cnt + MOE_BLOCK - 1) // MOE_BLOCK) * MOE_BLOCK
    pad_ends = jnp.cumsum(padded)
    pad_starts = pad_ends - padded
    n_blocks = -(-(t * TOP_K + N_EXPERTS * (MOE_BLOCK - 1)) // MOE_BLOCK)
    cap = n_blocks * MOE_BLOCK
    block_start = jnp.arange(n_blocks, dtype=I32) * MOE_BLOCK
    block_exp = jnp.minimum(jnp.sum(pad_ends[None, :] <= block_start[:, None], axis=1), N_EXPERTS - 1).astype(I32)
    n_used = (pad_ends[-1:] // MOE_BLOCK).astype(I32)
    dest = (pad_starts[idx[:, :TOP_K]] + rank[:, :TOP_K]).astype(I32).reshape(-1)

    xs = jnp.zeros((cap,) + _row_tiles(dm), F32)
    off = 0
    for (h1, hn, _), n in zip(parts, sizes):
        xs = _dispatch(dest[off * TOP_K:(off + n) * TOP_K], hn, xs)
        off += n
    b1g = b1[:, None, 0::2]
    b1l = b1[:, None, 1::2]
    eo = _experts(block_exp, n_used, xs, w1, b1g, b1l, w2, b2[:, None, :])
    outs = []
    off = 0
    for (h1, hn, _), n in zip(parts, sizes):
        outs.append(_combine(dest[off * TOP_K:(off + n) * TOP_K], gate[off:off + n], h1, normf_g, eo))
        off += n
    return outs


def _rope_tables(pos, half):
    inv = ROPE_BASE ** (-jnp.arange(half, dtype=F32) / half)
    ang = pos.astype(F32)[:, None] * inv[None, :]
    return jnp.cos(ang), jnp.sin(ang)


def kernel(x_prompt, x_sample, cache_k, cache_v, state_ret, cache_mem_k, cache_mem_v, page_table, mem_prompt,
           norm1_g, w_in, lambda_q1, lambda_k1, lambda_q2, lambda_k2, subln_g, norm_mem_g, w_mem_kv, w_branch,
           w_out, norm2_g, router_w, router_b, w1, b1, w2, b2, normf_g):
    b, s, dm = x_prompt.shape
    db, t, _ = x_sample.shape
    depth = w_in.shape[0]
    assert depth == 1, "final norm is fused into the last layer's MoE combine; one layer supported"
    assert b == 1
    page = cache_k.shape[2]
    past_len = page_table.shape[1] * page
    dk_ret = dm // H_RET
    n_mem = mem_prompt.shape[1]

    cos_p, sin_p = _rope_tables(jnp.arange(s), dk_ret // 2)
    cos_s, sin_s = _rope_tables(past_len + jnp.arange(t), dk_ret // 2)
    cos_s = jnp.tile(cos_s, (db, 1))
    sin_s = jnp.tile(sin_s, (db, 1))

    l = 0
    lam_init = 0.8 - 0.6 * math.exp(-0.3 * l)
    lam4 = jnp.stack([lambda_q1[l], lambda_k1[l], lambda_q2[l], lambda_k2[l]]).astype(F32)
    g1 = norm1_g[l][None, :]
    sub_g = subln_g[l][None, :]
    d3 = 3 * dm
    w_l = w_in[l]
    w_diff = w_l[:, :d3].astype(BF16)
    w_ret = w_l[:, d3:d3 + 4 * dm].astype(BF16)
    w_mg = w_l[:, d3 + 4 * dm:].astype(BF16)
    wb = w_branch[l].astype(BF16)
    wo = w_out[l].astype(BF16)
    g2 = norm2_g[l][None, :]
    rw = jnp.zeros((dm, V7X_LANES), F32).at[:, :N_EXPERTS].set(router_w[l]).astype(BF16)
    rb = jnp.full((1, V7X_LANES), NEG_INF, F32).at[0, :N_EXPERTS].set(router_b[l])

    xp = x_prompt.reshape(s, dm)
    q_d, k_f, v_f, k_b, v_t = _proj_diff(xp, g1, w_diff, BF16)
    y_d = _diff_attn_prompt(q_d, k_b, v_t, lam4, subln_g[l][:, None], lam_init)
    rq, rk, rv, rg = _proj_ret(xp, g1, w_ret, cos_p, sin_p)
    to3 = lambda a: a.reshape(1, s, dm)
    y_r, st_p = _retention(to3(rq), to3(rk), to3(rv), to3(rg), jnp.zeros((1, H_RET, dk_ret, dk_ret), F32),
                           chunk=_row_tile(s, RET_CHUNK), y_dtype=BF16, mm_dtype=BF16)
    mq, gt = _proj_mem_gate(xp, g1, w_mg, BF16)
    mkv = _mem_kv(mem_prompt.reshape(n_mem, dm), norm_mem_g[l][None, :], w_mem_kv[l].astype(BF16))
    mk, mv = mkv[:, :dm], mkv[:, dm:]
    y_m = _mem_attn(mq[None], mk[None], mv[None], BF16)
    part_p = _merge(xp, y_d, y_r.reshape(s, dm), y_m.reshape(s, dm), gt, wb, wo, g2, rw, rb)

    ns = db * t
    xs_ = x_sample.reshape(ns, dm)
    q_s, ks_f, vs_f, _, _ = _proj_diff(xs_, g1, w_diff, F32)
    k_s5 = ks_f.reshape(db, t, H_DIFF, D_V)
    v_s5 = vs_f.reshape(db, t, H_DIFF, D_V)
    y_ds = _diff_attn_sample(page_table, q_s.reshape(db, t, dm), k_s5, v_s5, cache_k, cache_v, l,
                             lam4, sub_g, lam_init)
    rq, rk, rv, rg = _proj_ret(xs_, g1, w_ret, cos_s, sin_s)
    tos = lambda a: a.reshape(db, t, dm)
    y_rs, st_s = _retention(tos(rq), tos(rk), tos(rv), tos(rg), state_ret[l].astype(F32),
                            chunk=t, y_dtype=F32, mm_dtype=F32)
    mq_s, gt_s = _proj_mem_gate(xs_, g1, w_mg, F32)
    y_ms = _mem_attn(mq_s.reshape(db, t, dm), cache_mem_k[l].reshape(db, n_mem, dm),
                     cache_mem_v[l].reshape(db, n_mem, dm), F32)
    part_s = _merge(xs_, y_ds.reshape(ns, dm), y_rs.reshape(ns, dm), y_ms.reshape(ns, dm), gt_s, wb, wo, g2, rw, rb)

    y_p, y_s = _moe_and_final_norm([part_p, part_s], None, w1[l], b1[l], w2[l], b2[l], normf_g[None, :])

    return (y_p.reshape(b, s, dm), y_s.reshape(db, t, dm),
            k_f.reshape(1, b, s, H_DIFF, D_V), v_f.reshape(1, b, s, H_DIFF, D_V),
            st_p.reshape(1, b, H_RET, dk_ret, dk_ret),
            mk.reshape(1, b, n_mem, H_MEM, dm // H_MEM), mv.reshape(1, b, n_mem, H_MEM, dm // H_MEM),
            k_s5[None], v_s5[None], st_s[None].astype(state_ret.dtype))
```

```python
import functools
import math

import jax
import jax.numpy as jnp
import numpy as np
from jax import lax
from jax.experimental import pallas as pl
from jax.experimental.pallas import tpu as pltpu

F32 = jnp.float32
BF16 = jnp.bfloat16
I32 = jnp.int32

H_DIFF = 8
D_QK = 64
D_V = 128
H_RET = 4
H_MEM = 4
N_BRANCH = 3
N_EXPERTS = 32
TOP_K = 4
SWIGLU_LIMIT = 7.0
SWIGLU_ALPHA = 1.702
ROPE_BASE = 10000.0
RMS_EPS = 1e-6

V7X_LANES = 128
V7X_VMEM_LIMIT_BYTES = 56 * 1024 * 1024

MOE_BLOCK = 256
DMA_ISSUE_UNROLL = 16
MOE_ROW_DMA_TOKENS = 512
SAMPLE_PAGES_PER_STEP = 16
SAMPLE_PAGES_PER_UPDATE = 4
RET_CHUNK = 256
NEG_INF = float("-inf")
DIFF_Q_SCALE = (D_QK ** -0.5) * math.log2(math.e)


def _cparams(sem):
    return pltpu.CompilerParams(dimension_semantics=sem, vmem_limit_bytes=V7X_VMEM_LIMIT_BYTES)


def _rms(x, g=None):
    y = x * lax.rsqrt(jnp.mean(x * x, axis=-1, keepdims=True) + RMS_EPS)
    return y if g is None else y * g


def _row_tiles(dm):
    return (dm // V7X_LANES, V7X_LANES)


def _row_tile(n, pref):
    t = min(n, pref)
    while n % t:
        t //= 2
    return t


def _proj_diff_kernel(x_ref, g_ref, w_ref, q_ref, kf_ref, vf_ref, kb_ref, vt_ref):
    xn = _rms(x_ref[...], g_ref[...]).astype(BF16)
    d = kf_ref.shape[-1]
    q = jnp.dot(xn, w_ref[:, 0:d], preferred_element_type=F32)
    q_ref[...] = (q * DIFF_Q_SCALE).astype(q_ref.dtype)
    k = jnp.dot(xn, w_ref[:, d:2 * d], preferred_element_type=F32)
    kf_ref[...] = k
    kb_ref[...] = k.astype(BF16)
    v = jnp.dot(xn, w_ref[:, 2 * d:3 * d], preferred_element_type=F32)
    vf_ref[...] = v
    vt_ref[...] = v.T.astype(BF16)


def _proj_diff(x, g, w, q_dtype):
    t, dm = x.shape
    tm = _row_tile(t, 512)
    row = lambda i: (i, 0)
    blk = pl.BlockSpec((tm, dm), row)
    return pl.pallas_call(
        _proj_diff_kernel,
        grid=(t // tm,),
        in_specs=[blk, pl.BlockSpec((1, dm), lambda i: (0, 0)), pl.BlockSpec(w.shape, lambda i: (0, 0))],
        out_specs=[blk] * 4 + [pl.BlockSpec((dm, tm), lambda i: (0, i))],
        out_shape=[jax.ShapeDtypeStruct((t, dm), q_dtype), jax.ShapeDtypeStruct((t, dm), F32),
                   jax.ShapeDtypeStruct((t, dm), F32), jax.ShapeDtypeStruct((t, dm), BF16),
                   jax.ShapeDtypeStruct((dm, t), BF16)],
        compiler_params=_cparams(("parallel",)),
        name="proj_diff",
    )(x, g, w)


def _proj_ret_kernel(x_ref, g_ref, w_ref, cos_ref, sin_ref, q_ref, k_ref, v_ref, gate_ref):
    xn = _rms(x_ref[...], g_ref[...]).astype(BF16)
    d = q_ref.shape[-1]
    dk = d // H_RET
    half = dk // 2
    cos = cos_ref[...]
    sin = sin_ref[...]

    def rope(u, scale):
        outs = []
        for h in range(H_RET):
            x1 = u[:, h * dk:h * dk + half]
            x2 = u[:, h * dk + half:(h + 1) * dk]
            outs.append((x1 * cos - x2 * sin) * scale)
            outs.append((x1 * sin + x2 * cos) * scale)
        return jnp.concatenate(outs, axis=-1)

    q_ref[...] = rope(jnp.dot(xn, w_ref[:, 0:d], preferred_element_type=F32), 1.0)
    k_ref[...] = rope(jnp.dot(xn, w_ref[:, d:2 * d], preferred_element_type=F32), dk ** -0.5)
    v_ref[...] = jnp.dot(xn, w_ref[:, 2 * d:3 * d], preferred_element_type=F32)
    gate_ref[...] = jnp.dot(xn, w_ref[:, 3 * d:4 * d], preferred_element_type=F32)


def _proj_ret(x, g, w, cos, sin):
    t, dm = x.shape
    tm = _row_tile(t, 512)
    row = lambda i: (i, 0)
    blk = pl.BlockSpec((tm, dm), row)
    tab = pl.BlockSpec((tm, cos.shape[1]), row)
    return pl.pallas_call(
        _proj_ret_kernel,
        grid=(t // tm,),
        in_specs=[blk, pl.BlockSpec((1, dm), lambda i: (0, 0)), pl.BlockSpec(w.shape, lambda i: (0, 0)), tab, tab],
        out_specs=[blk] * 4,
        out_shape=[jax.ShapeDtypeStruct((t, dm), F32)] * 4,
        compiler_params=_cparams(("parallel",)),
        name="proj_ret",
    )(x, g, w, cos, sin)


def _proj_mem_gate_kernel(x_ref, g_ref, w_ref, mq_ref, gt_ref):
    xn = _rms(x_ref[...], g_ref[...]).astype(BF16)
    d = mq_ref.shape[-1]
    mq = jnp.dot(xn, w_ref[:, 0:d], preferred_element_type=F32)
    mq_ref[...] = (mq * ((d // H_MEM) ** -0.5)).astype(mq_ref.dtype)
    gt_ref[...] = jnp.dot(xn, w_ref[:, d:], preferred_element_type=F32)


def _proj_mem_gate(x, g, w, mq_dtype):
    t, dm = x.shape
    tm = _row_tile(t, 512)
    row = lambda i: (i, 0)
    return pl.pallas_call(
        _proj_mem_gate_kernel,
        grid=(t // tm,),
        in_specs=[pl.BlockSpec((tm, dm), row), pl.BlockSpec((1, dm), lambda i: (0, 0)),
                  pl.BlockSpec(w.shape, lambda i: (0, 0))],
        out_specs=[pl.BlockSpec((tm, dm), row), pl.BlockSpec((tm, N_BRANCH * dm), row)],
        out_shape=[jax.ShapeDtypeStruct((t, dm), mq_dtype), jax.ShapeDtypeStruct((t, N_BRANCH * dm), F32)],
        compiler_params=_cparams(("parallel",)),
        name="proj_mem_gate",
    )(x, g, w)


def _lambda_value(lam_ref, lam_init):
    t = lam_ref[...]
    a = jnp.sum(t[0:1] * t[1:2], axis=-1, keepdims=True)
    b = jnp.sum(t[2:3] * t[3:4], axis=-1, keepdims=True)
    return jnp.exp(a) - jnp.exp(b) + lam_init


def _diff_attn_prompt_kernel(q_ref, k_ref, vt_ref, lam_ref, g_ref, o_ref, m_s, l_s, acc_s, sa_s, sb_s, mba_s, mbb_s,
                             *, tq, tk, rc, lam_init):
    qi = pl.program_id(1)
    q = q_ref[...]
    lane = lax.broadcasted_iota(I32, q.shape, 1)
    zero = jnp.zeros_like(q)
    qz = jnp.concatenate([jnp.where(lane < D_QK, q, zero), jnp.where(lane >= D_QK, q, zero)], axis=0)
    m_s[...] = jnp.full(m_s.shape, NEG_INF, F32)
    l_s[...] = jnp.zeros(l_s.shape, F32)
    acc_s[...] = jnp.zeros(acc_s.shape, F32)
    n_full = (qi * tq) // tk

    def scores(j, masked, s_buf, mb_buf):
        ks = k_ref[pl.ds(pl.multiple_of(j * tk, tk), tk), :]
        s = lax.dot_general(ks, qz, (((1,), (1,)), ((), ())), preferred_element_type=F32)
        if masked:
            kpos = lax.broadcasted_iota(I32, s.shape, 0) + j * tk
            col = lax.broadcasted_iota(I32, s.shape, 1)
            qpos = jnp.where(col >= tq, col - tq, col) + qi * tq
            s = jnp.where(kpos <= qpos, s, NEG_INF)
        s_buf[...] = s
        mb_buf[...] = jnp.max(s, axis=0, keepdims=True)

    def absorb(s_buf, mb_buf, vblock):
        m_old = m_s[...]
        m_new = jnp.maximum(m_old, mb_buf[...])
        alpha = jnp.exp2(m_old - m_new)
        part = jnp.zeros((8, 2 * tq), F32)
        vstart = pl.multiple_of(vblock * tk, tk)
        pv = None
        for r in range(tk // rc):
            p = jnp.exp2(s_buf[r * rc:(r + 1) * rc, :] - m_new)
            part = part + jnp.sum(p.reshape(rc // 8, 8, 2 * tq), axis=0)
            vt = vt_ref[:, pl.ds(vstart + r * rc, rc)]
            d = jnp.dot(vt, p.astype(BF16), preferred_element_type=F32)
            pv = d if pv is None else pv + d
        l_s[...] = alpha * l_s[...] + jnp.sum(part, axis=0, keepdims=True)
        acc_s[...] = alpha * acc_s[...] + pv
        m_s[...] = m_new

    scores(n_full, True, sa_s, mba_s)
    n_pairs = n_full // 2

    def body(i, c):
        scores(2 * i, False, sb_s, mbb_s)
        absorb(sa_s, mba_s, jnp.where(i == 0, n_full, 2 * i - 1))
        scores(2 * i + 1, False, sa_s, mba_s)
        absorb(sb_s, mbb_s, 2 * i)
        return c

    lax.fori_loop(0, n_pairs, body, 0)
    pending = jnp.where(n_pairs == 0, n_full, 2 * n_pairs - 1)

    @pl.when(n_full % 2 == 1)
    def _():
        scores(n_full - 1, False, sb_s, mbb_s)
        absorb(sa_s, mba_s, pending)
        absorb(sb_s, mbb_s, n_full - 1)

    @pl.when(n_full % 2 == 0)
    def _():
        absorb(sa_s, mba_s, pending)

    o = acc_s[...] / l_s[...]
    lam = _lambda_value(lam_ref, lam_init)
    od = o[:, :tq] - lam * o[:, tq:]
    y = od * lax.rsqrt(jnp.mean(od * od, axis=0, keepdims=True) + RMS_EPS) * g_ref[...] * (1.0 - lam_init)
    o_ref[...] = y.T.astype(o_ref.dtype)


def _diff_attn_prompt(q, k, vt, lam4, subln_g_col, lam_init):
    s, dm = q.shape
    tq = _row_tile(s, 512)
    tk = _row_tile(s, 1024)
    rc = _row_tile(tk, 256)
    kern = functools.partial(_diff_attn_prompt_kernel, tq=tq, tk=tk, rc=rc, lam_init=lam_init)
    row_stat = pltpu.VMEM((1, 2 * tq), F32)
    score_buf = pltpu.VMEM((tk, 2 * tq), F32)
    return pl.pallas_call(
        kern,
        grid=(H_DIFF, s // tq),
        in_specs=[pl.BlockSpec((tq, D_V), lambda h, i: (i, h)),
                  pl.BlockSpec((s, D_V), lambda h, i: (0, h)),
                  pl.BlockSpec((D_V, s), lambda h, i: (h, 0)),
                  pl.BlockSpec(lam4.shape, lambda h, i: (0, 0)),
                  pl.BlockSpec((D_V, 1), lambda h, i: (0, 0))],
        out_specs=pl.BlockSpec((tq, D_V), lambda h, i: (i, h)),
        out_shape=jax.ShapeDtypeStruct((s, dm), BF16),
        scratch_shapes=[row_stat, row_stat, pltpu.VMEM((D_V, 2 * tq), F32),
                        score_buf, score_buf, row_stat, row_stat],
        compiler_params=_cparams(("parallel", "parallel")),
        name="diff_attn_prompt",
    )(q, k, vt, lam4, subln_g_col)


def _diff_attn_sample_kernel(pt_ref, q_ref, kn_ref, vn_ref, lam_ref, g_ref, *rest, pps, lam_init):
    k_refs = rest[:pps]
    v_refs = rest[pps:2 * pps]
    o_ref, qmat_s, bias_s, m_s, l_s, acc_s, s_s = rest[2 * pps:]
    j = pl.program_id(1)
    nq = q_ref.shape[1]
    pc = bias_s.shape[1]

    @pl.when(j == 0)
    def _():
        qb = q_ref[0]
        lane = lax.broadcasted_iota(I32, (nq, D_V), 1)
        parts = []
        for h in range(H_DIFF):
            blk = qb[:, h * D_V:(h + 1) * D_V]
            parts.append(jnp.where(lane < D_QK, blk, 0.0))
            parts.append(jnp.where(lane >= D_QK, blk, 0.0))
        qmat_s[...] = jnp.concatenate(parts, axis=0).astype(BF16)
        r = lax.broadcasted_iota(I32, bias_s.shape, 0)
        c = lax.broadcasted_iota(I32, bias_s.shape, 1)
        bias_s[...] = jnp.where((c % H_DIFF) == (r // (2 * nq)), 0.0, NEG_INF)
        m_s[...] = jnp.full(m_s.shape, NEG_INF, F32)
        l_s[...] = jnp.zeros(l_s.shape, F32)
        acc_s[...] = jnp.zeros(acc_s.shape, F32)

    def flat(page):
        return page.reshape(page.shape[0] * page.shape[1], page.shape[2]).astype(BF16)

    qmat = qmat_s[...]
    grp = s_s.shape[1] // pc
    for g0 in range(0, pps, grp):
        mb = None
        for r in range(grp):
            s = lax.dot_general(qmat, flat(k_refs[g0 + r][0, 0]), (((1,), (1,)), ((), ())),
                                preferred_element_type=F32) + bias_s[...]
            s_s[:, r * pc:(r + 1) * pc] = s
            mr = jnp.max(s, axis=-1, keepdims=True)
            mb = mr if mb is None else jnp.maximum(mb, mr)
        m_old = m_s[...]
        m_new = jnp.maximum(m_old, mb)
        alpha = jnp.exp2(m_old - m_new)
        lsum = None
        pv = None
        for r in range(grp):
            p = jnp.exp2(s_s[:, r * pc:(r + 1) * pc] - m_new)
            ls = jnp.sum(p, axis=-1, keepdims=True)
            lsum = ls if lsum is None else lsum + ls
            d = jnp.dot(p.astype(BF16), flat(v_refs[g0 + r][0, 0]), preferred_element_type=F32)
            pv = d if pv is None else pv + d
        l_s[...] = alpha * l_s[...] + lsum
        acc_s[...] = alpha * acc_s[...] + pv
        m_s[...] = m_new

    @pl.when(j == pl.num_programs(1) - 1)
    def _():
        s = lax.dot_general(qmat, flat(kn_ref[0]), (((1,), (1,)), ((), ())), preferred_element_type=F32)
        r = lax.broadcasted_iota(I32, s.shape, 0)
        c = lax.broadcasted_iota(I32, s.shape, 1)
        keep = ((c % H_DIFF) == (r // (2 * nq))) & ((c // H_DIFF) <= (r % nq))
        s = jnp.where(keep, s, NEG_INF)
        m_o = m_s[...]
        m_n = jnp.maximum(m_o, jnp.max(s, axis=-1, keepdims=True))
        al = jnp.exp2(m_o - m_n)
        p = jnp.exp2(s - m_n)
        l = al * l_s[...] + jnp.sum(p, axis=-1, keepdims=True)
        acc = al * acc_s[...] + jnp.dot(p.astype(BF16), flat(vn_ref[0]), preferred_element_type=F32)
        o = acc / l
        lam = _lambda_value(lam_ref, lam_init)
        g = g_ref[...]
        outs = []
        for h in range(H_DIFF):
            base = h * 2 * nq
            od = o[base:base + nq] - lam * o[base + nq:base + 2 * nq]
            outs.append(_rms(od, g) * (1.0 - lam_init))
        o_ref[0] = jnp.concatenate(outs, axis=-1)


def _diff_attn_sample(page_table, q, k_new, v_new, cache_k, cache_v, layer, lam4, subln_g, lam_init,
                      pps=SAMPLE_PAGES_PER_STEP, grp=SAMPLE_PAGES_PER_UPDATE):
    b, nq, dm = q.shape
    n_pages = page_table.shape[1]
    while n_pages % pps:
        pps //= 2
    grp = min(grp, pps)
    page_shape = (1, 1) + cache_k.shape[2:]
    page_specs = [pl.BlockSpec(page_shape, functools.partial(
        lambda bi, j, pt, r: (layer, pt[bi, j * pps + r], 0, 0, 0), r=r)) for r in range(pps)]
    rows = H_DIFF * 2 * nq
    cols = cache_k.shape[2] * cache_k.shape[3]
    kern = functools.partial(_diff_attn_sample_kernel, pps=pps, lam_init=lam_init)
    per_seq3 = lambda bi, j, pt: (bi, 0, 0)
    per_seq4 = lambda bi, j, pt: (bi, 0, 0, 0)
    const2 = lambda bi, j, pt: (0, 0)
    grid_spec = pltpu.PrefetchScalarGridSpec(
        num_scalar_prefetch=1,
        grid=(b, n_pages // pps),
        in_specs=[pl.BlockSpec((1, nq, dm), per_seq3),
                  pl.BlockSpec((1,) + k_new.shape[1:], per_seq4),
                  pl.BlockSpec((1,) + v_new.shape[1:], per_seq4),
                  pl.BlockSpec(lam4.shape, const2),
                  pl.BlockSpec((1, D_V), const2)] + page_specs + page_specs,
        out_specs=pl.BlockSpec((1, nq, dm), per_seq3),
        scratch_shapes=[pltpu.VMEM((rows, D_V), BF16), pltpu.VMEM((rows, cols), F32),
                        pltpu.VMEM((rows, 1), F32), pltpu.VMEM((rows, 1), F32), pltpu.VMEM((rows, D_V), F32),
                        pltpu.VMEM((rows, grp * cols), F32)],
    )
    return pl.pallas_call(
        kern,
        grid_spec=grid_spec,
        out_shape=jax.ShapeDtypeStruct((b, nq, dm), F32),
        compiler_params=_cparams(("parallel", "arbitrary")),
        name="diff_attn_sample",
    )(page_table, q, k_new, v_new, lam4, subln_g, *([cache_k] * pps), *([cache_v] * pps))


def _retention_log_decay(h):
    return float(np.log(np.float32(1.0) - np.exp2(np.float32(-5.0 - h))))


def _retention_kernel(q_ref, k_ref, v_ref, gate_ref, s0_ref, y_ref, sout_ref, state_s, *, mm_dtype):
    c = pl.program_id(1)

    @pl.when(c == 0)
    def _():
        state_s[...] = s0_ref[0]

    cl = q_ref.shape[1]
    dk = state_s.shape[1]
    ii = lax.broadcasted_iota(I32, (cl, cl), 0)
    jj = lax.broadcasted_iota(I32, (cl, cl), 1)
    dist = (ii - jj).astype(F32)
    ri = lax.broadcasted_iota(I32, (cl, 1), 0).astype(F32)
    for h in range(H_RET):
        log_g = _retention_log_decay(h)
        sl = slice(h * dk, (h + 1) * dk)
        q = q_ref[0, :, sl]
        k = k_ref[0, :, sl]
        decay = jnp.where(dist >= 0, jnp.exp(jnp.maximum(dist, 0.0) * log_g), 0.0)
        row_decay = jnp.exp((ri + 1.0) * log_g)
        col_decay = jnp.exp((cl - 1.0 - ri) * log_g)
        qm = q.astype(mm_dtype)
        vm = v_ref[0, :, sl].astype(mm_dtype)
        state = state_s[h]
        sc = lax.dot_general(qm, k.astype(mm_dtype), (((1,), (1,)), ((), ())), preferred_element_type=F32) * decay
        o = jnp.dot(sc.astype(mm_dtype), vm, preferred_element_type=F32)
        o = o + jnp.dot(qm, state.astype(mm_dtype), preferred_element_type=F32) * row_decay
        kw = (k * col_decay).astype(mm_dtype)
        upd = lax.dot_general(kw, vm, (((0,), (0,)), ((), ())), preferred_element_type=F32)
        state_s[h] = math.exp(cl * log_g) * state + upd
        g = gate_ref[0, :, sl]
        y_ref[0, :, sl] = (g * jax.nn.sigmoid(g) * _rms(o)).astype(y_ref.dtype)

    @pl.when(c == pl.num_programs(1) - 1)
    def _():
        sout_ref[0] = state_s[...]


def _retention(q, k, v, gate, state0, chunk, y_dtype, mm_dtype):
    b, s, dm = q.shape
    nc = s // chunk
    blk = pl.BlockSpec((1, chunk, dm), lambda bi, c: (bi, c, 0))
    st = pl.BlockSpec((1,) + state0.shape[1:], lambda bi, c: (bi, 0, 0, 0))
    return pl.pallas_call(
        functools.partial(_retention_kernel, mm_dtype=mm_dtype),
        grid=(b, nc),
        in_specs=[blk, blk, blk, blk, st],
        out_specs=[blk, st],
        out_shape=[jax.ShapeDtypeStruct((b, s, dm), y_dtype), jax.ShapeDtypeStruct(state0.shape, F32)],
        scratch_shapes=[pltpu.VMEM(state0.shape[1:], F32)],
        compiler_params=_cparams(("parallel", "arbitrary")),
        name="retention",
    )(q, k, v, gate, state0)


def _mem_kv_kernel(mem_ref, g_ref, w_ref, o_ref):
    xn = _rms(mem_ref[...], g_ref[...]).astype(BF16)
    o_ref[...] = jnp.dot(xn, w_ref[...], preferred_element_type=F32)


def _mem_kv(mem, g, w):
    m, dm = mem.shape
    n = w.shape[1]
    return pl.pallas_call(
        _mem_kv_kernel,
        grid=(n // dm,),
        in_specs=[pl.BlockSpec((m, dm), lambda i: (0, 0)), pl.BlockSpec((1, dm), lambda i: (0, 0)),
                  pl.BlockSpec((dm, dm), lambda i: (0, i))],
        out_specs=pl.BlockSpec((m, dm), lambda i: (0, i)),
        out_shape=jax.ShapeDtypeStruct((m, n), F32),
        compiler_params=_cparams(("parallel",)),
        name="mem_kv",
    )(mem, g, w)


def _mem_attn_kernel(q_ref, mk_ref, mv_ref, o_ref):
    q = q_ref[0].astype(BF16)
    mk = mk_ref[0].astype(BF16)
    mv = mv_ref[0].astype(BF16)
    d = q.shape[-1] // H_MEM
    outs = []
    for h in range(H_MEM):
        sl = slice(h * d, (h + 1) * d)
        s = lax.dot_general(q[:, sl], mk[:, sl], (((1,), (1,)), ((), ())), preferred_element_type=F32)
        m = jnp.max(s, axis=-1, keepdims=True)
        p = jnp.exp(s - m)
        p = p / jnp.sum(p, axis=-1, keepdims=True)
        outs.append(jnp.dot(p.astype(BF16), mv[:, sl], preferred_element_type=F32))
    o_ref[0] = jnp.concatenate(outs, axis=-1).astype(o_ref.dtype)


def _mem_attn(q, mk, mv, y_dtype):
    b, t, dm = q.shape
    tm = _row_tile(t, 512)
    qb = pl.BlockSpec((1, tm, dm), lambda bi, i: (bi, i, 0))
    mb = pl.BlockSpec((1,) + mk.shape[1:], lambda bi, i: (bi, 0, 0))
    return pl.pallas_call(
        _mem_attn_kernel,
        grid=(b, t // tm),
        in_specs=[qb, mb, mb],
        out_specs=qb,
        out_shape=jax.ShapeDtypeStruct((b, t, dm), y_dtype),
        compiler_params=_cparams(("parallel", "parallel")),
        name="mem_attn",
    )(q, mk, mv)


def _merge_kernel(x_ref, yd_ref, yr_ref, ym_ref, gt_ref, wb_ref, wo_ref, g2_ref, rw_ref, rb_ref,
                  h1_ref, hn_ref, lg_ref):
    dm = x_ref.shape[-1]
    mixed = jnp.zeros(x_ref.shape, F32)
    for i, y_ref in enumerate((yd_ref, yr_ref, ym_ref)):
        proj = jnp.dot(y_ref[...].astype(BF16), wb_ref[i], preferred_element_type=F32)
        mixed = mixed + jax.nn.sigmoid(gt_ref[:, i * dm:(i + 1) * dm]) * proj
    h1 = x_ref[...] + jnp.dot(mixed.astype(BF16), wo_ref[...], preferred_element_type=F32)
    h1_ref[...] = h1
    hn = _rms(h1, g2_ref[...])
    hn_ref[...] = hn.reshape(hn_ref.shape)
    lg_ref[...] = jnp.dot(hn.astype(BF16), rw_ref[...], preferred_element_type=F32) + rb_ref[...]


def _merge(x, yd, yr, ym, gt, wb, wo, g2, rw, rb):
    t, dm = x.shape
    tm = _row_tile(t, 512)
    row = lambda i: (i, 0)
    blk = pl.BlockSpec((tm, dm), row)
    c2 = lambda i: (0, 0)
    return pl.pallas_call(
        _merge_kernel,
        grid=(t // tm,),
        in_specs=[blk, blk, blk, blk, pl.BlockSpec((tm, N_BRANCH * dm), row),
                  pl.BlockSpec(wb.shape, lambda i: (0, 0, 0)), pl.BlockSpec(wo.shape, c2),
                  pl.BlockSpec((1, dm), c2), pl.BlockSpec(rw.shape, c2), pl.BlockSpec(rb.shape, c2)],
        out_specs=[blk, pl.BlockSpec((tm,) + _row_tiles(dm), lambda i: (i, 0, 0)), pl.BlockSpec((tm, V7X_LANES), row)],
        out_shape=[jax.ShapeDtypeStruct((t, dm), F32), jax.ShapeDtypeStruct((t,) + _row_tiles(dm), F32),
                   jax.ShapeDtypeStruct((t, V7X_LANES), F32)],
        compiler_params=_cparams(("parallel",)),
        name="merge",
    )(x, yd, yr, ym, gt, wb, wo, g2, rw, rb)


def _route_kernel(lg_ref, idx_ref, gate_ref, rank_ref, cnt_ref, carry_s):
    i = pl.program_id(0)

    @pl.when(i == 0)
    def _():
        carry_s[...] = jnp.zeros(carry_s.shape, F32)

    l = lg_ref[...]
    tm = l.shape[0]
    lane = lax.broadcasted_iota(I32, l.shape, 1)
    vals, idxs, hots = [], [], []
    for _ in range(TOP_K):
        m = jnp.max(l, axis=-1, keepdims=True)
        ik = jnp.min(jnp.where(l == m, lane, V7X_LANES), axis=-1, keepdims=True)
        hot = lane == ik
        vals.append(m)
        idxs.append(ik)
        hots.append(hot)
        l = jnp.where(hot, NEG_INF, l)
    es = [jnp.exp(v - vals[0]) for v in vals]
    den = es[0] + es[1] + es[2] + es[3]
    picked = jnp.zeros(l.shape, F32)
    for hot in hots:
        picked = picked + jnp.where(hot, 1.0, 0.0)
    r = lax.broadcasted_iota(I32, (tm, tm), 0)
    c = lax.broadcasted_iota(I32, (tm, tm), 1)
    before = jnp.where(c < r, 1.0, 0.0).astype(BF16)
    cum = jnp.dot(before, picked.astype(BF16), preferred_element_type=F32) + carry_s[0:1, :]
    idx_o = jnp.zeros(l.shape, I32)
    gate_o = jnp.zeros(l.shape, F32)
    rank_o = jnp.zeros(l.shape, I32)
    for k in range(TOP_K):
        rk = jnp.sum(jnp.where(hots[k], cum, 0.0), axis=-1, keepdims=True).astype(I32)
        idx_o = jnp.where(lane == k, idxs[k], idx_o)
        gate_o = jnp.where(lane == k, es[k] / den, gate_o)
        rank_o = jnp.where(lane == k, rk, rank_o)
    idx_ref[...] = idx_o
    gate_ref[...] = gate_o
    rank_ref[...] = rank_o
    carry_s[...] = carry_s[...] + jnp.sum(picked, axis=0, keepdims=True)
    cnt_ref[...] = carry_s[...]


def _route(logits):
    t = logits.shape[0]
    tm = _row_tile(t, 512)
    row = lambda i: (i, 0)
    blk = pl.BlockSpec((tm, V7X_LANES), row)
    return pl.pallas_call(
        _route_kernel,
        grid=(t // tm,),
        in_specs=[blk],
        out_specs=[blk, blk, blk, pl.BlockSpec((8, V7X_LANES), lambda i: (0, 0))],
        out_shape=[jax.ShapeDtypeStruct((t, V7X_LANES), I32), jax.ShapeDtypeStruct((t, V7X_LANES), F32),
                   jax.ShapeDtypeStruct((t, V7X_LANES), I32), jax.ShapeDtypeStruct((8, V7X_LANES), F32)],
        scratch_shapes=[pltpu.VMEM((8, V7X_LANES), F32)],
        compiler_params=_cparams(("arbitrary",)),
        name="route",
    )(logits)


def _row_copy(src, src_row, dst, dst_row, sem):
    return pltpu.make_async_copy(src.at[pl.ds(src_row, 1)], dst.at[pl.ds(dst_row, 1)], sem)


def _dispatch_kernel(dest_ref, hn_ref, xs_in_ref, xs_ref, sem):
    del xs_in_ref
    tm = hn_ref.shape[0]

    def issue(g, c):
        for u in range(DMA_ISSUE_UNROLL):
            r = g * DMA_ISSUE_UNROLL + u
            for k in range(TOP_K):
                _row_copy(hn_ref, r, xs_ref, dest_ref[r * TOP_K + k], sem).start(priority=(u * TOP_K + k) % 2)
        return c

    lax.fori_loop(0, tm // DMA_ISSUE_UNROLL, issue, 0)
    for k in range(TOP_K):
        pltpu.make_async_copy(hn_ref, xs_ref.at[pl.ds(0, tm)], sem).wait()


def _dispatch(dest_flat, hn, xs):
    t = hn.shape[0]
    tm = _row_tile(t, MOE_ROW_DMA_TOKENS)
    return pl.pallas_call(
        _dispatch_kernel,
        grid=(t // tm,),
        in_specs=[pl.BlockSpec((tm * TOP_K,), lambda i: (i,), memory_space=pltpu.SMEM),
                  pl.BlockSpec((tm,) + hn.shape[1:], lambda i: (i, 0, 0)),
                  pl.BlockSpec(memory_space=pl.ANY)],
        out_specs=pl.BlockSpec(memory_space=pl.ANY),
        out_shape=jax.ShapeDtypeStruct(xs.shape, xs.dtype),
        scratch_shapes=[pltpu.SemaphoreType.DMA(())],
        input_output_aliases={2: 0},
        compiler_params=_cparams(("arbitrary",)),
        name="moe_dispatch",
    )(dest_flat, hn, xs)


def _expert_kernel(be_ref, nu_ref, xs_ref, w1_ref, b1g_ref, b1l_ref, w2_ref, b2_ref, o_ref, w1g_s, w1l_s, w2_s):
    i = pl.program_id(0)
    e = be_ref[i]
    prev = be_ref[jnp.maximum(i - 1, 0)]

    @pl.when((i == 0) | (e != prev))
    def _():
        cw = 2 * V7X_LANES
        r = lax.broadcasted_iota(I32, (cw, cw), 0)
        c = lax.broadcasted_iota(I32, (cw, cw), 1)
        src_col = jnp.where(c < V7X_LANES, 2 * c, 2 * (c - V7X_LANES) + 1)
        sel = jnp.where(r == src_col, 1.0, 0.0).astype(BF16)
        for j in range(w1_ref.shape[2] // cw):
            t = jnp.dot(w1_ref[0, :, j * cw:(j + 1) * cw].astype(BF16), sel, preferred_element_type=F32)
            w1g_s[:, j * V7X_LANES:(j + 1) * V7X_LANES] = t[:, :V7X_LANES].astype(BF16)
            w1l_s[:, j * V7X_LANES:(j + 1) * V7X_LANES] = t[:, V7X_LANES:].astype(BF16)
        w2_s[...] = w2_ref[0].astype(BF16)

    @pl.when(i < nu_ref[0])
    def _():
        x = xs_ref[...].reshape(xs_ref.shape[0], w2_s.shape[1]).astype(BF16)
        ug = jnp.dot(x, w1g_s[...], preferred_element_type=F32) + b1g_ref[0]
        ul = jnp.dot(x, w1l_s[...], preferred_element_type=F32) + b1l_ref[0]
        x_glu = jnp.minimum(ug, SWIGLU_LIMIT)
        x_lin = jnp.clip(ul, -SWIGLU_LIMIT, SWIGLU_LIMIT)
        act = x_glu * jax.nn.sigmoid(SWIGLU_ALPHA * x_glu) * (x_lin + 1.0)
        out = jnp.dot(act.astype(BF16), w2_s[...], preferred_element_type=F32) + b2_ref[0]
        o_ref[...] = out.reshape(o_ref.shape)

    @pl.when(i >= nu_ref[0])
    def _():
        o_ref[...] = jnp.zeros(o_ref.shape, F32)


def _experts(block_exp, n_used, xs, w1, b1g, b1l, w2, b2):
    cap = xs.shape[0]
    dm = w2.shape[2]
    dff = w2.shape[1]
    n_blocks = cap // MOE_BLOCK
    rows_blk = (MOE_BLOCK,) + xs.shape[1:]
    row_in = lambda i, be, nu: (jnp.minimum(i, nu[0] - 1), 0, 0)
    row_out = lambda i, be, nu: (i, 0, 0)
    wsel = lambda i, be, nu: (be[i], 0, 0)
    grid_spec = pltpu.PrefetchScalarGridSpec(
        num_scalar_prefetch=2,
        grid=(n_blocks,),
        in_specs=[pl.BlockSpec(rows_blk, row_in),
                  pl.BlockSpec((1, dm, 2 * dff), wsel),
                  pl.BlockSpec((1, 1, dff), wsel), pl.BlockSpec((1, 1, dff), wsel),
                  pl.BlockSpec((1, dff, dm), wsel), pl.BlockSpec((1, 1, dm), wsel)],
        out_specs=pl.BlockSpec(rows_blk, row_out),
        scratch_shapes=[pltpu.VMEM((dm, dff), BF16), pltpu.VMEM((dm, dff), BF16), pltpu.VMEM((dff, dm), BF16)],
    )
    return pl.pallas_call(
        _expert_kernel,
        grid_spec=grid_spec,
        out_shape=jax.ShapeDtypeStruct(xs.shape, F32),
        compiler_params=_cparams(("arbitrary",)),
        name="moe_experts",
    )(block_exp, n_used, xs, w1, b1g, b1l, w2, b2)


def _combine_kernel(dest_ref, gate_ref, h1_ref, gf_ref, eo_ref, y_ref, buf, sem):
    tm = h1_ref.shape[0]

    def issue(g, c):
        for u in range(DMA_ISSUE_UNROLL):
            r = g * DMA_ISSUE_UNROLL + u
            for k in range(TOP_K):
                _row_copy(eo_ref, dest_ref[r * TOP_K + k], buf.at[k], r, sem).start(priority=(u * TOP_K + k) % 2)
        return c

    lax.fori_loop(0, tm // DMA_ISSUE_UNROLL, issue, 0)
    for k in range(TOP_K):
        pltpu.make_async_copy(eo_ref.at[pl.ds(0, tm)], buf.at[k], sem).wait()
    gate = gate_ref[...]
    y = h1_ref[...]
    for k in range(TOP_K):
        y = y + gate[:, k:k + 1] * buf[k].reshape(y.shape)
    y_ref[...] = _rms(y, gf_ref[...])


def _combine(dest_flat, gate, h1, gf, eo):
    t, dm = h1.shape
    tm = _row_tile(t, MOE_ROW_DMA_TOKENS)
    row = lambda i: (i, 0)
    return pl.pallas_call(
        _combine_kernel,
        grid=(t // tm,),
        in_specs=[pl.BlockSpec((tm * TOP_K,), lambda i: (i,), memory_space=pltpu.SMEM),
                  pl.BlockSpec((tm, V7X_LANES), row), pl.BlockSpec((tm, dm), row),
                  pl.BlockSpec((1, dm), lambda i: (0, 0)), pl.BlockSpec(memory_space=pl.ANY)],
        out_specs=pl.BlockSpec((tm, dm), row),
        out_shape=jax.ShapeDtypeStruct((t, dm), F32),
        scratch_shapes=[pltpu.VMEM((TOP_K, tm) + eo.shape[1:], F32), pltpu.SemaphoreType.DMA(())],
        compiler_params=_cparams(("arbitrary",)),
        name="moe_combine",
    )(dest_flat, gate, h1, gf, eo)


def _moe_and_final_norm(parts, router_b_unused, w1, b1, w2, b2, normf_g):
    del router_b_unused
    sizes = [p[0].shape[0] for p in parts]
    t = sum(sizes)
    dm = parts[0][0].shape[1]
    logits = jnp.concatenate([p[2] for p in parts], axis=0)
    idx, gate, rank, counts = _route(logits)
    cnt = counts[0, :N_EXPERTS].astype(I32)
    padded = ((cnt + MOE_BLOCK - 1) // MOE_BLOCK) * MOE_BLOCK
    pad_ends = jnp.cumsum(padded)
    pad_starts = pad_ends - padded
    n_blocks = -(-(t * TOP_K + N_EXPERTS * (MOE_BLOCK - 1)) // MOE_BLOCK)
    cap = n_blocks * MOE_BLOCK
    block_start = jnp.arange(n_blocks, dtype=I32) * MOE_BLOCK
    block_exp = jnp.minimum(jnp.sum(pad_ends[None, :] <= block_start[:, None], axis=1), N_EXPERTS - 1).astype(I32)
    n_used = (pad_ends[-1:] // MOE_BLOCK).astype(I32)
    dest = (pad_starts[idx[:, :TOP_K]] + rank[:, :TOP_K]).astype(I32).reshape(-1)

    xs = jnp.zeros((cap,) + _row_tiles(dm), F32)
    off = 0
    for (h1, hn, _), n in zip(parts, sizes):
        xs = _dispatch(dest[off * TOP_K:(off + n) * TOP_K], hn, xs)
        off += n
    b1g = b1[:, None, 0::2]
    b1l = b1[:, None, 1::2]
    eo = _experts(block_exp, n_used, xs, w1, b1g, b1l, w2, b2[:, None, :])
    outs = []
    off = 0
    for (h1, hn, _), n in zip(parts, sizes):
        outs.append(_combine(dest[off * TOP_K:(off + n) * TOP_K], gate[off:off + n], h1, normf_g, eo))
        off += n
    return outs


def _rope_tables(pos, half):
    inv = ROPE_BASE ** (-jnp.arange(half, dtype=F32) / half)
    ang = pos.astype(F32)[:, None] * inv[None, :]
    return jnp.cos(ang), jnp.sin(ang)


def kernel(x_prompt, x_sample, cache_k, cache_v, state_ret, cache_mem_k, cache_mem_v, page_table, mem_prompt,
           norm1_g, w_in, lambda_q1, lambda_k1, lambda_q2, lambda_k2, subln_g, norm_mem_g, w_mem_kv, w_branch,
           w_out, norm2_g, router_w, router_b, w1, b1, w2, b2, normf_g):
    b, s, dm = x_prompt.shape
    db, t, _ = x_sample.shape
    depth = w_in.shape[0]
    assert depth == 1, "final norm is fused into the last layer's MoE combine; one layer supported"
    assert b == 1
    page = cache_k.shape[2]
    past_len = page_table.shape[1] * page
    dk_ret = dm // H_RET
    n_mem = mem_prompt.shape[1]

    cos_p, sin_p = _rope_tables(jnp.arange(s), dk_ret // 2)
    cos_s, sin_s = _rope_tables(past_len + jnp.arange(t), dk_ret // 2)
    cos_s = jnp.tile(cos_s, (db, 1))
    sin_s = jnp.tile(sin_s, (db, 1))

    l = 0
    lam_init = 0.8 - 0.6 * math.exp(-0.3 * l)
    lam4 = jnp.stack([lambda_q1[l], lambda_k1[l], lambda_q2[l], lambda_k2[l]]).astype(F32)
    g1 = norm1_g[l][None, :]
    sub_g = subln_g[l][None, :]
    d3 = 3 * dm
    w_l = w_in[l]
    w_diff = w_l[:, :d3].astype(BF16)
    w_ret = w_l[:, d3:d3 + 4 * dm].astype(BF16)
    w_mg = w_l[:, d3 + 4 * dm:].astype(BF16)
    wb = w_branch[l].astype(BF16)
    wo = w_out[l].astype(BF16)
    g2 = norm2_g[l][None, :]
    rw = jnp.zeros((dm, V7X_LANES), F32).at[:, :N_EXPERTS].set(router_w[l]).astype(BF16)
    rb = jnp.full((1, V7X_LANES), NEG_INF, F32).at[0, :N_EXPERTS].set(router_b[l])

    xp = x_prompt.reshape(s, dm)
    q_d, k_f, v_f, k_b, v_t = _proj_diff(xp, g1, w_diff, BF16)
    y_d = _diff_attn_prompt(q_d, k_b, v_t, lam4, subln_g[l][:, None], lam_init)
    rq, rk, rv, rg = _proj_ret(xp, g1, w_ret, cos_p, sin_p)
    to3 = lambda a: a.reshape(1, s, dm)
    y_r, st_p = _retention(to3(rq), to3(rk), to3(rv), to3(rg), jnp.zeros((1, H_RET, dk_ret, dk_ret), F32),
                           chunk=_row_tile(s, RET_CHUNK), y_dtype=BF16, mm_dtype=BF16)
    mq, gt = _proj_mem_gate(xp, g1, w_mg, BF16)
    mkv = _mem_kv(mem_prompt.reshape(n_mem, dm), norm_mem_g[l][None, :], w_mem_kv[l].astype(BF16))
    mk, mv = mkv[:, :dm], mkv[:, dm:]
    y_m = _mem_attn(mq[None], mk[None], mv[None], BF16)
    part_p = _merge(xp, y_d, y_r.reshape(s, dm), y_m.reshape(s, dm), gt, wb, wo, g2, rw, rb)

    ns = db * t
    xs_ = x_sample.reshape(ns, dm)
    q_s, ks_f, vs_f, _, _ = _proj_diff(xs_, g1, w_diff, F32)
    k_s5 = ks_f.reshape(db, t, H_DIFF, D_V)
    v_s5 = vs_f.reshape(db, t, H_DIFF, D_V)
    y_ds = _diff_attn_sample(page_table, q_s.reshape(db, t, dm), k_s5, v_s5, cache_k, cache_v, l,
                             lam4, sub_g, lam_init)
    rq, rk, rv, rg = _proj_ret(xs_, g1, w_ret, cos_s, sin_s)
    tos = lambda a: a.reshape(db, t, dm)
    y_rs, st_s = _retention(tos(rq), tos(rk), tos(rv), tos(rg), state_ret[l].astype(F32),
                            chunk=t, y_dtype=F32, mm_dtype=F32)
    mq_s, gt_s = _proj_mem_gate(xs_, g1, w_mg, F32)
    y_ms = _mem_attn(mq_s.reshape(db, t, dm), cache_mem_k[l].reshape(db, n_mem, dm),
                     cache_mem_v[l].reshape(db, n_mem, dm), F32)
    part_s = _merge(xs_, y_ds.reshape(ns, dm), y_rs.reshape(ns, dm), y_ms.reshape(ns, dm), gt_s, wb, wo, g2, rw, rb)

    y_p, y_s = _moe_and_final_norm([part_p, part_s], None, w1[l], b1[l], w2[l], b2[l], normf_g[None, :])

    return (y_p.reshape(b, s, dm), y_s.reshape(db, t, dm),
            k_f.reshape(1, b, s, H_DIFF, D_V), v_f.reshape(1, b, s, H_DIFF, D_V),
            st_p.reshape(1, b, H_RET, dk_ret, dk_ret),
            mk.reshape(1, b, n_mem, H_MEM, dm // H_MEM), mv.reshape(1, b, n_mem, H_MEM, dm // H_MEM),
            k_s5[None], v_s5[None], st_s[None].astype(state_ret.dtype))
```
